```python
import math
import jax, jax.numpy as jnp
from jax import lax
import numpy as np

D_MODEL = 1024
BATCH = 32
SEQ = 2048
DEPTH = 4
DEC_BATCH = 8
DEC_SEQ = 16
PAST_LEN = 4096

CHUNK = 64
A_GROUPS = 4
A_GDIM = 64
A_WIDTH = A_GROUPS * A_GDIM
SGU_LEN = 128
B_HEADS = 4
B_DK = 64
B_DV = 2 * B_DK
B_WIDTH = B_HEADS * B_DV
B_QK = B_HEADS * 2 * B_DK
QBLK = 128
C_HEADS = 4
C_DK = 64
C_DV = 64
C_WIDTH = C_HEADS * C_DV
CONV_W = 4
C_QKV = C_HEADS * (2 * C_DK + C_DV)
D_MIX = A_WIDTH + B_WIDTH + C_WIDTH
P_A = 2 * A_WIDTH
P_B = 2 * B_QK + B_WIDTH
P_C = C_QKV + C_WIDTH + 2 * C_HEADS
P_IN = P_A + P_B + P_C
PEER_HEADS = 8
PEER_NKEYS = 128
PEER_N = PEER_NKEYS * PEER_NKEYS
PEER_DQ = 256
PEER_TOPK = 16
PEER_BLOCK = 512
DN_ALPHA = (2 * DEPTH) ** 0.25
DN_BETA = (8 * DEPTH) ** -0.25
LN_EPS = 1e-5

kernel_name = "hybrid_stream_encoder_step"


def layer_norm(x, g=None, b=None):
    xf = x.astype(jnp.float32)
    mu = jnp.mean(xf, -1, keepdims=True)
    var = jnp.mean(jnp.square(xf - mu), -1, keepdims=True)
    y = (xf - mu) * lax.rsqrt(var + LN_EPS)
    if g is not None:
        y = y * g.astype(jnp.float32) + b.astype(jnp.float32)
    return y.astype(x.dtype)


def rms_norm(x, g):
    xf = x.astype(jnp.float32)
    y = xf * lax.rsqrt(jnp.mean(xf * xf, -1, keepdims=True) + LN_EPS) * g.astype(jnp.float32)
    return y.astype(x.dtype)


def l2norm(x):
    return x * lax.rsqrt(jnp.sum(x * x, -1, keepdims=True) + 1e-6)


def alibi_slopes():
    return jnp.exp2(-(8.0 / B_HEADS) * jnp.arange(1, B_HEADS + 1, dtype=jnp.float32))


def spatial_gating(u, v, ln_g, ln_b, w_s, b_s):
    nb, t, _ = u.shape
    v = layer_norm(v.reshape(nb, t, A_GROUPS, A_GDIM), ln_g, ln_b)
    pad = -t % SGU_LEN
    vc = jnp.pad(v, ((0, 0), (0, pad), (0, 0), (0, 0))).reshape(nb, -1, SGU_LEN, A_GROUPS, A_GDIM)
    i = jnp.arange(SGU_LEN)
    mask = (i[None, :] // CHUNK) <= (i[:, None] // CHUNK)
    w = jnp.where(mask[None], w_s, 0.0).astype(vc.dtype)
    s = jnp.einsum('gij,bnjgd->bnigd', w, vc) + b_s.T[None, None, :, :, None]
    s = s.reshape(nb, -1, A_GROUPS, A_GDIM)[:, :t]
    y = (u.reshape(nb, t, A_GROUPS, A_GDIM) * s).reshape(nb, t, A_WIDTH)
    return y, v.reshape(nb, t, A_WIDTH)


def diff_core(q, k, v, qpos, kpos, lam):
    s = jnp.einsum('bqhmd,bkhmd->bhmqk', q, k, preferred_element_type=jnp.float32) * (B_DK ** -0.5)
    dist = jnp.abs(qpos[:, None] - kpos[None, :]).astype(jnp.float32)
    s = s - alibi_slopes()[None, :, None, None, None] * dist
    mask = (kpos[None, :] // CHUNK) <= (qpos[:, None] // CHUNK)
    s = jnp.where(mask, s, -jnp.inf)
    p = jax.nn.softmax(s, axis=-1)
    a = p[:, :, 0] - lam * p[:, :, 1]
    return jnp.einsum('bhqk,bkhd->bqhd', a.astype(v.dtype), v)


def diff_attention_prompt(q, k, v, lam):
    t = q.shape[1]
    pos = jnp.arange(t)
    outs = []
    for qb in range(t // QBLK):
        lo, hi = qb * QBLK, (qb + 1) * QBLK
        outs.append(diff_core(q[:, lo:hi], k[:, :hi], v[:, :hi], pos[lo:hi], pos[:hi], lam))
    return jnp.concatenate(outs, axis=1)


def diff_attention_cached(q, k, v, past_k, past_v, lam):
    p_len, t = past_k.shape[1], q.shape[1]
    kk = jnp.concatenate([past_k.astype(k.dtype), k], axis=1)
    vv = jnp.concatenate([past_v.astype(v.dtype), v], axis=1)
    return diff_core(q, kk, vv, p_len + jnp.arange(t), jnp.arange(p_len + t), lam)


def gated_delta_rule(q, k, v, g, beta, s0):
    f32 = jnp.float32
    nb, t, nh, _ = q.shape
    dv = v.shape[-1]
    pad = -t % CHUNK
    def padt(z):
        return jnp.pad(z.astype(f32), ((0, 0), (0, pad)) + ((0, 0),) * (z.ndim - 2))
    n = (t + pad) // CHUNK
    def blk(z):
        return jnp.moveaxis(padt(z).reshape(nb, n, CHUNK, nh, -1), 3, 1)
    q, k, v = blk(q), blk(k), blk(v)
    g, beta = blk(g)[..., 0], blk(beta)[..., 0]
    gc = jnp.cumsum(g, axis=-1)
    idx = jnp.arange(CHUNK)
    lower = idx[:, None] >= idx[None, :]
    strict = idx[:, None] > idx[None, :]
    gamma = jnp.where(lower, jnp.exp(jnp.where(lower, gc[..., :, None] - gc[..., None, :], 0.0)), 0.0)
    kb = k * beta[..., None]
    m = jnp.where(strict, jnp.einsum('bhnid,bhnjd->bhnij', kb, k) * gamma, 0.0)
    rhs = jnp.concatenate([v * beta[..., None], kb * jnp.exp(gc)[..., None]], axis=-1)
    sol = lax.linalg.triangular_solve(jnp.eye(CHUNK, dtype=f32) + m, rhs, left_side=True, lower=True)
    u, w = sol[..., :dv], sol[..., dv:]
    aqk = jnp.einsum('bhnid,bhnjd->bhnij', q, k) * gamma
    qg = q * jnp.exp(gc)[..., None]
    kdec = k * jnp.exp(gc[..., -1:] - gc)[..., None]
    glast = jnp.exp(gc[..., -1])

    def step(s, inp):
        u_n, w_n, qg_n, aqk_n, kdec_n, gl_n = inp
        v_new = u_n - jnp.einsum('bhcd,bhde->bhce', w_n, s)
        o_n = jnp.einsum('bhcd,bhde->bhce', qg_n, s) + jnp.einsum('bhij,bhje->bhie', aqk_n, v_new)
        s = s * gl_n[..., None, None] + jnp.einsum('bhcd,bhce->bhde', kdec_n, v_new)
        return s, o_n

    xs = tuple(jnp.moveaxis(z, 2, 0) for z in (u, w, qg, aqk, kdec, glast))
    s_fin, o = lax.scan(step, s0.astype(f32), xs)
    o = jnp.transpose(o, (1, 0, 3, 2, 4)).reshape(nb, n * CHUNK, nh, dv)[:, :t]
    return o, s_fin


def gated_deltanet(pc, conv_w, a_log, dt_bias, norm_g, conv_state, s0):
    nb, t, _ = pc.shape
    qkv = pc[..., :C_QKV]
    z = pc[..., C_QKV:C_QKV + C_WIDTH]
    a = pc[..., C_QKV + C_WIDTH:C_QKV + C_WIDTH + C_HEADS]
    bb = pc[..., C_QKV + C_WIDTH + C_HEADS:]
    if conv_state is None:
        conv_state = jnp.zeros((nb, CONV_W - 1, C_QKV), qkv.dtype)
    if s0 is None:
        s0 = jnp.zeros((nb, C_HEADS, C_DK, C_DV), jnp.float32)
    xp = jnp.concatenate([conv_state.astype(qkv.dtype), qkv], axis=1)
    conv_tail = xp[:, -(CONV_W - 1):]
    conv = xp[:, :t] * conv_w[0]
    for j in range(1, CONV_W):
        conv = conv + xp[:, j:j + t] * conv_w[j]
    conv = jax.nn.silu(conv).astype(jnp.float32)
    q = conv[..., :C_HEADS * C_DK].reshape(nb, t, C_HEADS, C_DK)
    k = conv[..., C_HEADS * C_DK:2 * C_HEADS * C_DK].reshape(nb, t, C_HEADS, C_DK)
    v = conv[..., 2 * C_HEADS * C_DK:].reshape(nb, t, C_HEADS, C_DV)
    q = l2norm(q) * (C_DK ** -0.5)
    k = l2norm(k)
    g = -jnp.exp(a_log.astype(jnp.float32)) * jax.nn.softplus(a.astype(jnp.float32) + dt_bias.astype(jnp.float32))
    beta = jax.nn.sigmoid(bb.astype(jnp.float32))
    o, s_fin = gated_delta_rule(q, k, v, g, beta, s0)
    o = rms_norm(o.astype(pc.dtype), norm_g) * jax.nn.silu(z.reshape(nb, t, C_HEADS, C_DV))
    return o.reshape(nb, t, C_WIDTH), s_fin, conv_tail


def peer_block(h, w_query, sub_keys, expert_u, expert_v):
    n = h.shape[0]
    q = (h @ w_query).reshape(n, PEER_HEADS, 2, PEER_DQ // 2)
    s = jnp.einsum('nhpd,hpkd->nhpk', q, sub_keys, preferred_element_type=jnp.float32)
    top_s, top_i = lax.top_k(s, PEER_TOPK)
    cand = top_s[:, :, 0, :, None] + top_s[:, :, 1, None, :]
    best_s, best_c = lax.top_k(cand.reshape(n, PEER_HEADS, PEER_TOPK * PEER_TOPK), PEER_TOPK)
    i1 = jnp.take_along_axis(top_i[:, :, 0], best_c // PEER_TOPK, axis=-1)
    i2 = jnp.take_along_axis(top_i[:, :, 1], best_c % PEER_TOPK, axis=-1)
    ids = (i1 * PEER_NKEYS + i2).reshape(n, PEER_HEADS * PEER_TOPK)
    gates = jax.nn.softmax(best_s, axis=-1).reshape(n, PEER_HEADS * PEER_TOPK)
    act = jax.nn.gelu(jnp.einsum('nkd,nd->nk', expert_u[ids], h))
    return jnp.einsum('nk,nkd->nd', (gates * act).astype(h.dtype), expert_v[ids])


def peer(h, w_query, sub_keys, expert_u, expert_v):
    nb, t, d = h.shape
    flat = h.reshape(nb * t, d)
    n = flat.shape[0]
    flat = jnp.pad(flat, ((0, -n % PEER_BLOCK), (0, 0)))
    out = lax.map(lambda hb: peer_block(hb, w_query, sub_keys, expert_u, expert_v),
                  flat.reshape(-1, PEER_BLOCK, d))
    return out.reshape(-1, d)[:n].reshape(nb, t, d)


def run_group(x, c, past_k, past_v, gdn_state, conv_state,
              w_ada, b_ada, w_in, sgu_ln_g, sgu_ln_b, sgu_w, sgu_b,
              lam_q1, lam_k1, lam_q2, lam_k2, diff_norm_g,
              conv_w, gdn_a_log, gdn_dt_bias, gdn_norm_g,
              w_out, ln1_g, ln1_b, peer_wq, peer_keys, expert_u, expert_v, ln2_g, ln2_b):
    nb, t, _ = x.shape
    cached = past_k is not None
    new_k, new_v, new_s, new_conv, new_sgu_v = [], [], [], [], []
    for l in range(DEPTH):
        mod = (jax.nn.silu(c) @ w_ada[l] + b_ada[l]).reshape(nb, 6, 1, D_MODEL)
        shift1, scale1, gate1, shift2, scale2, gate2 = (mod[:, i] for i in range(6))
        h = layer_norm(x) * (1 + scale1) + shift1
        p = h @ w_in[l]
        pa, pb, pc = p[..., :P_A], p[..., P_A:P_A + P_B], p[..., P_A + P_B:]
        ya, va = spatial_gating(jax.nn.gelu(pa[..., :A_WIDTH]), jax.nn.gelu(pa[..., A_WIDTH:]),
                                sgu_ln_g[l], sgu_ln_b[l], sgu_w[l], sgu_b[l])
        qb = pb[..., :B_QK].reshape(nb, t, B_HEADS, 2, B_DK)
        kb = pb[..., B_QK:2 * B_QK].reshape(nb, t, B_HEADS, 2, B_DK)
        vb = pb[..., 2 * B_QK:].reshape(nb, t, B_HEADS, B_DV)
        lam_init = 0.8 - 0.6 * math.exp(-0.3 * l)
        lam = (jnp.exp(jnp.sum(lam_q1[l].astype(jnp.float32) * lam_k1[l].astype(jnp.float32)))
               - jnp.exp(jnp.sum(lam_q2[l].astype(jnp.float32) * lam_k2[l].astype(jnp.float32)))
               + lam_init)
        if cached:
            ob = diff_attention_cached(qb, kb, vb, past_k[l].reshape(nb, -1, B_HEADS, 2, B_DK), past_v[l], lam)
        else:
            ob = diff_attention_prompt(qb, kb, vb, lam)
        yb = (rms_norm(ob, diff_norm_g[l]) * (1.0 - lam_init)).reshape(nb, t, B_WIDTH)
        yc, s_fin, conv_tail = gated_deltanet(pc, conv_w[l], gdn_a_log[l], gdn_dt_bias[l], gdn_norm_g[l],
                                              conv_state[l] if cached else None,
                                              gdn_state[l] if cached else None)
        y = jnp.concatenate([ya, yb, yc], axis=-1) @ w_out[l]
        x = layer_norm(DN_ALPHA * x + gate1 * y, ln1_g[l], ln1_b[l])
        h2 = layer_norm(x) * (1 + scale2) + shift2
        f = peer(h2, peer_wq[l], peer_keys[l], expert_u[l], expert_v[l])
        x = layer_norm(DN_ALPHA * x + gate2 * f, ln2_g[l], ln2_b[l])
        new_k.append(kb.reshape(nb, t, B_HEADS, 2 * B_DK))
        new_v.append(vb)
        new_s.append(s_fin)
        new_conv.append(conv_tail)
        if cached:
            new_sgu_v.append(va)
    sgu_state = jnp.stack(new_sgu_v) if cached else None
    return (x, jnp.stack(new_k), jnp.stack(new_v), jnp.stack(new_s), jnp.stack(new_conv), sgu_state)


def setup_inputs(seed: int = 0) -> dict:
    key = jax.random.key(seed)
    k = jax.random.split(key, 34)
    f32 = jnp.float32
    def nrm(kk, shape, scale):
        return jax.random.normal(kk, shape, f32) * scale
    dt = jnp.exp(jax.random.uniform(k[22], (DEPTH, C_HEADS), f32, math.log(1e-3), math.log(1e-1)))
    return {
        "x_prompt": nrm(k[0], (BATCH, SEQ, D_MODEL), 1.0),
        "x_sample": nrm(k[1], (DEC_BATCH, DEC_SEQ, D_MODEL), 1.0),
        "cache_k": nrm(k[2], (DEPTH, DEC_BATCH, PAST_LEN, B_HEADS, 2 * B_DK), 1.0),
        "cache_v": nrm(k[3], (DEPTH, DEC_BATCH, PAST_LEN, B_HEADS, B_DV), 1.0),
        "state_gdn": nrm(k[4], (DEPTH, DEC_BATCH, C_HEADS, C_DK, C_DV), 0.3),
        "state_conv": nrm(k[5], (DEPTH, DEC_BATCH, CONV_W - 1, C_QKV), 1.0),
        "c_prompt": nrm(k[6], (BATCH, D_MODEL), 1.0),
        "c_sample": nrm(k[7], (DEC_BATCH, D_MODEL), 1.0),
        "w_ada": nrm(k[8], (DEPTH, D_MODEL, 6 * D_MODEL), 0.5 * D_MODEL ** -0.5),
        "b_ada": nrm(k[9], (DEPTH, 6 * D_MODEL), 0.02),
        "w_in": nrm(k[10], (DEPTH, D_MODEL, P_IN), D_MODEL ** -0.5),
        "sgu_ln_g": 1.0 + nrm(k[11], (DEPTH, A_GROUPS, A_GDIM), 0.02),
        "sgu_ln_b": nrm(k[12], (DEPTH, A_GROUPS, A_GDIM), 0.02),
        "sgu_w": nrm(k[13], (DEPTH, A_GROUPS, SGU_LEN, SGU_LEN), 0.5 * SGU_LEN ** -0.5),
        "sgu_b": 1.0 + nrm(k[14], (DEPTH, A_GROUPS, SGU_LEN), 0.1),
        "lam_q1": nrm(k[15], (DEPTH, B_DK), 0.1),
        "lam_k1": nrm(k[16], (DEPTH, B_DK), 0.1),
        "lam_q2": nrm(k[17], (DEPTH, B_DK), 0.1),
        "lam_k2": nrm(k[18], (DEPTH, B_DK), 0.1),
        "diff_norm_g": 1.0 + nrm(k[19], (DEPTH, B_DV), 0.02),
        "conv_w": nrm(k[20], (DEPTH, CONV_W, C_QKV), CONV_W ** -0.5),
        "gdn_a_log": jnp.log(jax.random.uniform(k[21], (DEPTH, C_HEADS), f32, 1.0, 16.0)),
        "gdn_dt_bias": dt + jnp.log(-jnp.expm1(-dt)),
        "gdn_norm_g": 1.0 + nrm(k[23], (DEPTH, C_DV), 0.02),
        "w_out": nrm(k[24], (DEPTH, D_MIX, D_MODEL), DN_BETA * D_MIX ** -0.5),
        "ln1_g": 1.0 + nrm(k[25], (DEPTH, D_MODEL), 0.02),
        "ln1_b": nrm(k[26], (DEPTH, D_MODEL), 0.02),
        "peer_wq": nrm(k[27], (DEPTH, D_MODEL, PEER_HEADS * PEER_DQ), D_MODEL ** -0.5),
        "peer_keys": nrm(k[28], (DEPTH, PEER_HEADS, 2, PEER_NKEYS, PEER_DQ // 2), (PEER_DQ // 2) ** -0.5),
        "expert_u": nrm(k[29], (DEPTH, PEER_N, D_MODEL), D_MODEL ** -0.5),
        "expert_v": nrm(k[30], (DEPTH, PEER_N, D_MODEL), DN_BETA),
        "ln2_g": 1.0 + nrm(k[31], (DEPTH, D_MODEL), 0.02),
        "ln2_b": nrm(k[32], (DEPTH, D_MODEL), 0.02),
    }


def reference(x_prompt, x_sample, cache_k, cache_v, state_gdn, state_conv, c_prompt, c_sample,
              w_ada, b_ada, w_in, sgu_ln_g, sgu_ln_b, sgu_w, sgu_b,
              lam_q1, lam_k1, lam_q2, lam_k2, diff_norm_g,
              conv_w, gdn_a_log, gdn_dt_bias, gdn_norm_g,
              w_out, ln1_g, ln1_b, peer_wq, peer_keys, expert_u, expert_v, ln2_g, ln2_b):
    weights = (w_ada, b_ada, w_in, sgu_ln_g, sgu_ln_b, sgu_w, sgu_b,
               lam_q1, lam_k1, lam_q2, lam_k2, diff_norm_g,
               conv_w, gdn_a_log, gdn_dt_bias, gdn_norm_g,
               w_out, ln1_g, ln1_b, peer_wq, peer_keys, expert_u, expert_v, ln2_g, ln2_b)
    y_prompt, new_k_prompt, new_v_prompt, new_gdn_prompt, new_conv_prompt, _ = run_group(
        x_prompt, c_prompt, None, None, None, None, *weights)
    y_sample, new_k_sample, new_v_sample, new_gdn_sample, new_conv_sample, new_sgu_v_sample = run_group(
        x_sample, c_sample, cache_k, cache_v, state_gdn, state_conv, *weights)
    return (y_prompt, y_sample, new_k_prompt, new_v_prompt, new_gdn_prompt, new_conv_prompt,
            new_k_sample, new_v_sample, new_gdn_sample, new_conv_sample, new_sgu_v_sample)
```

```python
import functools
import math

import jax
import jax.numpy as jnp
from jax import lax
from jax.experimental import pallas as pl
from jax.experimental.pallas import tpu as pltpu

F32 = jnp.float32
BF16 = jnp.bfloat16

D_MODEL = 1024
DEPTH = 4
CHUNK = 64
A_GROUPS = 4
A_GDIM = 64
A_WIDTH = 256
SGU_LEN = 128
B_HEADS = 4
B_DK = 64
B_DV = 128
B_WIDTH = 512
B_QK = 512
C_HEADS = 4
C_DK = 64
C_DV = 64
C_WIDTH = 256
CONV_W = 4
C_QKV = 768
P_MAIN = 3072
P_IN = 3080
PEER_HEADS = 8
PEER_NKEYS = 128
PEER_DQ = 256
PEER_TOPK = 16
PEER_SLOTS = PEER_HEADS * PEER_TOPK
DN_ALPHA = (2 * DEPTH) ** 0.25
LN_EPS = 1e-5
NEG = -1e30

LANES = 128
VMEM_LIMIT = 56 * 1024 * 1024

NN = (((1,), (0,)), ((), ()))
NT = (((1,), (1,)), ((), ()))


def _cparams(*sem):
    return pltpu.CompilerParams(dimension_semantics=sem, vmem_limit_bytes=VMEM_LIMIT)


def _dot(a, b, dims=NN):
    return lax.dot_general(a, b, dims, preferred_element_type=F32)


def _split(x, n):
    parts = []
    r = x
    for _ in range(n - 1):
        p = r.astype(BF16)
        parts.append(p)
        r = r - p.astype(F32)
    parts.append(r.astype(BF16))
    return parts


def _dot3(a, b, dims=NN):
    ah, al = _split(a, 2)
    bh, bl = _split(b, 2)
    return _dot(ah, bh, dims) + (_dot(ah, bl, dims) + _dot(al, bh, dims))


def _dot_exact_rhs(x, m_bf16, n=3, dims=NN):
    parts = _split(x, n)
    acc = _dot(parts[-1], m_bf16, dims)
    for p in parts[-2::-1]:
        acc = acc + _dot(p, m_bf16, dims)
    return acc


def _dot_exact_lhs(m_bf16, x, n=3, dims=NN):
    parts = _split(x, n)
    acc = _dot(m_bf16, parts[-1], dims)
    for p in parts[-2::-1]:
        acc = acc + _dot(m_bf16, p, dims)
    return acc


def _ln(x):
    mu = jnp.mean(x, axis=-1, keepdims=True)
    d = x - mu
    var = jnp.mean(d * d, axis=-1, keepdims=True)
    return d * lax.rsqrt(var + LN_EPS)


def _iota(shape, dim):
    return lax.broadcasted_iota(jnp.int32, shape, dim)


def _ada_kernel(c_ref, w_ref, b_ref, o_ref):
    c = c_ref[...]
    s = c * jax.nn.sigmoid(c)
    o_ref[0] = _dot3(s, w_ref[0]) + b_ref[0]


def ada_mod(c_all, w_ada, b_ada):
    nbt = c_all.shape[0]
    tn = 1536
    return pl.pallas_call(
        _ada_kernel,
        grid=(DEPTH, 6 * D_MODEL // tn),
        in_specs=[
            pl.BlockSpec((nbt, D_MODEL), lambda l, j: (0, 0)),
            pl.BlockSpec((1, D_MODEL, tn), lambda l, j: (l, 0, j)),
            pl.BlockSpec((1, 1, tn), lambda l, j: (l, 0, j)),
        ],
        out_specs=pl.BlockSpec((1, nbt, tn), lambda l, j: (l, 0, j)),
        out_shape=jax.ShapeDtypeStruct((DEPTH, nbt, 6 * D_MODEL), F32),
        compiler_params=_cparams("parallel", "parallel"),
        name="ada_mod",
    )(c_all, w_ada, b_ada.reshape(DEPTH, 1, 6 * D_MODEL))


def _in_kernel(x_ref, mod_ref, w_ref, wab_ref, pa_ref, q_ref, k_ref, v_ref, pc_ref, pab_ref):
    h = _ln(x_ref[...]) * (1.0 + mod_ref[0, 1:2, :]) + mod_ref[0, 0:1, :]
    hb = h.astype(BF16)
    pa_ref[...] = _dot(hb, w_ref[:, 0:512])
    q_ref[...] = _dot(hb, w_ref[:, 512:1024])
    k_ref[...] = _dot(hb, w_ref[:, 1024:1536])
    v_ref[...] = _dot(hb, w_ref[:, 1536:2048])
    pc_ref[...] = _dot(hb, w_ref[:, 2048:3072])
    pab_ref[...] = _dot(hb, wab_ref[...])


def in_proj(x, mod, w_main, w_ab, t):
    n = x.shape[0]
    tm = min(512, t)
    per = t // tm
    row = lambda i: (i, 0)
    fixed = lambda i: (0, 0)
    sds = lambda w: jax.ShapeDtypeStruct((n, w), F32)
    return pl.pallas_call(
        _in_kernel,
        grid=(n // tm,),
        in_specs=[
            pl.BlockSpec((tm, D_MODEL), row),
            pl.BlockSpec((1, 6, D_MODEL), lambda i: (i // per, 0, 0)),
            pl.BlockSpec((D_MODEL, P_MAIN), fixed),
            pl.BlockSpec((D_MODEL, LANES), fixed),
        ],
        out_specs=[pl.BlockSpec((tm, 512), row)] * 4 + [pl.BlockSpec((tm, 1024), row), pl.BlockSpec((tm, LANES), row)],
        out_shape=[sds(512)] * 4 + [sds(1024), sds(LANES)],
        compiler_params=_cparams("parallel"),
        name="in_proj",
    )(x, mod, w_main, w_ab)


def _group_ones(width, gdim):
    r = _iota((width, width), 0) // gdim
    c = _iota((width, width), 1) // gdim
    return jnp.where(r == c, 1.0, 0.0).astype(BF16)


def _sgu_kernel(pa_ref, g_ref, b_ref, w_ref, bs_ref, ya_ref, va_ref, *, blk, nblk):
    pa = pa_ref[...]
    u = jax.nn.gelu(pa[:, :A_WIDTH])
    v = jax.nn.gelu(pa[:, A_WIDTH:])
    ones = _group_ones(A_WIDTH, A_GDIM)
    mean = _dot_exact_rhs(v, ones) * (1.0 / A_GDIM)
    d = v - mean
    var = _dot_exact_rhs(d * d, ones) * (1.0 / A_GDIM)
    vn = d * lax.rsqrt(var + LN_EPS) * g_ref[...] + b_ref[...]
    va_ref[...] = vn
    ri = _iota((blk, blk), 0) // CHUNK
    ci = _iota((blk, blk), 1) // CHUNK
    lane_group = _iota((blk, A_WIDTH), 1) // A_GDIM
    wm = [jnp.where(ci <= ri, w_ref[g], 0.0).astype(BF16) for g in range(A_GROUPS)]
    for r in range(nblk):
        vb = vn[r * blk:(r + 1) * blk, :].astype(BF16)
        s = jnp.zeros((blk, A_WIDTH), F32)
        for g in range(A_GROUPS):
            s = jnp.where(lane_group == g, _dot(wm[g], vb), s)
        ya_ref[r * blk:(r + 1) * blk, :] = u[r * blk:(r + 1) * blk, :] * (s + bs_ref[...])


def sgu(pa, ln_g, ln_b, w_s, b_s, t):
    n = pa.shape[0]
    blk = min(SGU_LEN, t)
    assert t % blk == 0
    tm = min(512, n)
    w = w_s[:, :blk, :blk]
    bs = jnp.repeat(b_s.T[:blk], A_GDIM, axis=1)
    row = lambda i: (i, 0)
    return pl.pallas_call(
        functools.partial(_sgu_kernel, blk=blk, nblk=tm // blk),
        grid=(n // tm,),
        in_specs=[
            pl.BlockSpec((tm, 2 * A_WIDTH), row),
            pl.BlockSpec((1, A_WIDTH), lambda i: (0, 0)),
            pl.BlockSpec((1, A_WIDTH), lambda i: (0, 0)),
            pl.BlockSpec((A_GROUPS, blk, blk), lambda i: (0, 0, 0)),
            pl.BlockSpec((blk, A_WIDTH), lambda i: (0, 0)),
        ],
        out_specs=[pl.BlockSpec((tm, A_WIDTH), row)] * 2,
        out_shape=[jax.ShapeDtypeStruct((n, A_WIDTH), F32)] * 2,
        compiler_params=_cparams("parallel"),
        name="sgu",
    )(pa, ln_g.reshape(1, A_WIDTH), ln_b.reshape(1, A_WIDTH), w, bs)


def _attn_kernel(lam_ref, slope_ref, q_ref, k_ref, v_ref, g_ref, o_ref, *, tq, tk, q_off, klen, out_scale):
    h = pl.program_id(1)
    qi = pl.program_id(2)
    rows = 2 * tq
    q = q_ref[0] * (B_DK ** -0.5)
    lane = _iota((tq, B_DV), 1)
    qq = jnp.concatenate([jnp.where(lane < B_DK, q, 0.0), jnp.where(lane >= B_DK, q, 0.0)], axis=0).astype(BF16)
    qpos = q_off + qi * tq + (_iota((rows, tk), 0) & (tq - 1))
    col = _iota((rows, tk), 1)
    slope = slope_ref[h]
    limit = jnp.minimum(klen, ((q_off + (qi + 1) * tq - 1) // CHUNK + 1) * CHUNK)
    nkb = (limit + tk - 1) // tk

    def body(kb, carry):
        m, l, acc = carry
        start = pl.multiple_of(kb * tk, tk)
        kblk = k_ref[0, pl.ds(start, tk), :].astype(BF16)
        vblk = v_ref[0, pl.ds(start, tk), :].astype(BF16)
        s = _dot(qq, kblk, NT)
        kpos = kb * tk + col
        s = s - slope * jnp.abs(qpos - kpos).astype(F32)
        kchunk = jnp.where(kpos < klen, kpos // CHUNK, jnp.int32(1 << 30))
        s = jnp.where(kchunk <= (qpos // CHUNK), s, NEG)
        m_new = jnp.maximum(m, jnp.max(s, axis=1, keepdims=True))
        alpha = jnp.exp(m - m_new)
        p = jnp.exp(s - m_new)
        l = alpha * l + jnp.sum(p, axis=1, keepdims=True)
        acc = alpha * acc + _dot(p.astype(BF16), vblk)
        return m_new, l, acc

    init = (jnp.full((rows, 1), NEG, F32), jnp.zeros((rows, 1), F32), jnp.zeros((rows, B_DV), F32))
    m, l, acc = lax.fori_loop(0, nkb, body, init)
    o = acc / l
    ob = o[:tq] - lam_ref[0] * o[tq:]
    ms = jnp.mean(ob * ob, axis=-1, keepdims=True)
    o_ref[0] = ob * lax.rsqrt(ms + LN_EPS) * g_ref[...] * out_scale


def diff_attn(q, k, v, lam, norm_g, out_scale, *, q_off, klen, tq, tk):
    nb, t, _ = q.shape
    tkp = k.shape[1]
    slopes = jnp.exp2(-(8.0 / B_HEADS) * jnp.arange(1, B_HEADS + 1, dtype=F32))
    smem = pl.BlockSpec(memory_space=pltpu.SMEM)
    return pl.pallas_call(
        functools.partial(_attn_kernel, tq=tq, tk=tk, q_off=q_off, klen=klen, out_scale=out_scale),
        grid=(nb, B_HEADS, t // tq),
        in_specs=[
            smem, smem,
            pl.BlockSpec((1, tq, B_DV), lambda b, h, i: (b, i, h)),
            pl.BlockSpec((1, tkp, B_DV), lambda b, h, i: (b, 0, h)),
            pl.BlockSpec((1, tkp, B_DV), lambda b, h, i: (b, 0, h)),
            pl.BlockSpec((1, B_DV), lambda b, h, i: (0, 0)),
        ],
        out_specs=pl.BlockSpec((1, tq, B_DV), lambda b, h, i: (b, i, h)),
        out_shape=jax.ShapeDtypeStruct((nb, t, B_WIDTH), F32),
        compiler_params=_cparams("parallel", "parallel", "arbitrary"),
        name="diff_attn",
    )(lam.reshape(1), slopes, q, k, v, norm_g.reshape(1, B_DV))


def _gdn_kernel(pc_ref, pab_ref, cst_ref, cw_ref, alog_ref, dtb_ref, ng_ref, s0_ref,
                yc_ref, sfin_ref, prev_scr, *, t_valid):
    c = pl.program_id(1)
    W = C_WIDTH

    @pl.when(c == 0)
    def _():
        prev_scr[...] = cst_ref[0]
        sfin_ref[0] = s0_ref[0]

    qkv = pc_ref[0, :, 0:C_QKV]
    z = pc_ref[0, :, C_QKV:C_QKV + W]
    ext = jnp.concatenate([prev_scr[...], qkv], axis=0)
    prev_scr[...] = qkv[CHUNK - 8:, :]
    conv = ext[5:5 + CHUNK] * cw_ref[0:1, :]
    for j in range(1, CONV_W):
        conv = conv + ext[5 + j:5 + j + CHUNK] * cw_ref[j:j + 1, :]
    conv = conv * jax.nn.sigmoid(conv)
    q, k, v = conv[:, 0:W], conv[:, W:2 * W], conv[:, 2 * W:3 * W]

    head_ones = _group_ones(W, C_DK)
    q = q * lax.rsqrt(_dot_exact_rhs(q * q, head_ones) + 1e-6) * (C_DK ** -0.5)
    k = k * lax.rsqrt(_dot_exact_rhs(k * k, head_ones) + 1e-6)

    pab = pab_ref[0]
    g_cols = -jnp.exp(alog_ref[...]) * jax.nn.softplus(pab + dtb_ref[...])
    b_cols = jax.nn.sigmoid(pab)
    er = _iota((LANES, W), 0)
    eh = _iota((LANES, W), 1) // C_DK
    g = _dot_exact_rhs(g_cols, jnp.where(er == eh, 1.0, 0.0).astype(BF16))
    beta = _dot_exact_rhs(b_cols, jnp.where(er == eh + C_HEADS, 1.0, 0.0).astype(BF16))
    if t_valid % CHUNK:
        valid = (c * CHUNK + _iota((CHUNK, W), 0)) < t_valid
        q, k, v = (jnp.where(valid, a, 0.0) for a in (q, k, v))
        g = jnp.where(valid, g, 0.0)
        beta = jnp.where(valid, beta, 0.0)

    tri = jnp.where(_iota((CHUNK, CHUNK), 0) >= _iota((CHUNK, CHUNK), 1), 1.0, 0.0).astype(BF16)
    gc = _dot_exact_lhs(tri, g)
    eg = jnp.exp(gc)
    glast = gc[CHUNK - 1:CHUNK, :]
    kdec = k * jnp.exp(glast - gc)
    qg = q * eg
    kb = k * beta
    kbe = kb * eg
    vb = v * beta

    R = C_HEADS * CHUNK
    rr = _iota((R, W), 0)
    cc = _iota((R, W), 1)
    same = (rr // CHUNK) == (cc // C_DK)
    cfar = jnp.where(same, cc, jnp.int32(1 << 30))
    lower = rr >= cfar
    strict = rr > cfar

    def xl(a):
        return jnp.where(same, jnp.concatenate([a] * C_HEADS, axis=0), 0.0)

    kx = xl(k)
    kk = _dot3(xl(kb), kx, NT)
    qk = _dot3(xl(q), kx, NT)
    pick = jnp.where((cc % C_DK) == 0, 1.0, 0.0).astype(BF16)
    gcx = xl(gc)
    g_row = _dot_exact_rhs(gcx, pick, dims=NT)
    g_col = _dot_exact_lhs(pick, gcx, dims=NT)
    gamma = jnp.where(lower, jnp.exp(jnp.where(lower, g_row - g_col, 0.0)), 0.0)
    nm = jnp.where(strict, -(kk * gamma), 0.0)
    aqk = qk * gamma

    tinv = jnp.where(rr == cc, 1.0, 0.0) + nm
    pw = nm
    for _ in range(5):
        pw = _dot3(pw, pw)
        tinv = tinv + _dot3(pw, tinv)
    u_x = _dot3(tinv, xl(vb))
    w_x = _dot3(tinv, xl(kbe))

    s = sfin_ref[0]
    v_new = u_x - _dot3(w_x, s)
    o_x = _dot3(xl(qg), s) + _dot3(aqk, v_new)
    sfin_ref[0] = s * jnp.exp(glast) + _dot3(xl(kdec).T, v_new)
    o = o_x[0:CHUNK]
    for hh in range(1, C_HEADS):
        o = o + o_x[hh * CHUNK:(hh + 1) * CHUNK]
    ms = _dot_exact_rhs(o * o, head_ones) * (1.0 / C_DV)
    yc_ref[0] = o * lax.rsqrt(ms + LN_EPS) * ng_ref[...] * (z * jax.nn.sigmoid(z))


def gdn(pc, pab, conv_state, s0, conv_w, a_log, dt_bias, norm_g, t_valid):
    nb, tp, _ = pc.shape
    cst = jnp.pad(conv_state, ((0, 0), (8 - (CONV_W - 1), 0), (0, 0)))
    cw = jnp.pad(conv_w, ((0, 8 - CONV_W), (0, 0)))
    pad_row = lambda a: jnp.pad(a.reshape(1, C_HEADS), ((0, 0), (0, LANES - C_HEADS)))
    eye = jnp.eye(C_HEADS, dtype=F32)
    s0_bd = jnp.einsum('bhde,hg->bhdge', s0, eye).reshape(nb, C_HEADS * C_DK, C_HEADS * C_DV)
    full = lambda b, c: (0, 0)
    yc, sfin = pl.pallas_call(
        functools.partial(_gdn_kernel, t_valid=t_valid),
        grid=(nb, tp // CHUNK),
        in_specs=[
            pl.BlockSpec((1, CHUNK, 1024), lambda b, c: (b, c, 0)),
            pl.BlockSpec((1, CHUNK, LANES), lambda b, c: (b, c, 0)),
            pl.BlockSpec((1, 8, C_QKV), lambda b, c: (b, 0, 0)),
            pl.BlockSpec((8, C_QKV), full),
            pl.BlockSpec((1, LANES), full),
            pl.BlockSpec((1, LANES), full),
            pl.BlockSpec((1, C_WIDTH), full),
            pl.BlockSpec((1, C_WIDTH, C_WIDTH), lambda b, c: (b, 0, 0)),
        ],
        out_specs=[
            pl.BlockSpec((1, CHUNK, C_WIDTH), lambda b, c: (b, c, 0)),
            pl.BlockSpec((1, C_WIDTH, C_WIDTH), lambda b, c: (b, 0, 0)),
        ],
        out_shape=[
            jax.ShapeDtypeStruct((nb, tp, C_WIDTH), F32),
            jax.ShapeDtypeStruct((nb, C_WIDTH, C_WIDTH), F32),
        ],
        scratch_shapes=[pltpu.VMEM((8, C_QKV), F32)],
        compiler_params=_cparams("parallel", "arbitrary"),
        name="gdn",
    )(pc, pab, cst, cw, pad_row(a_log), pad_row(dt_bias), jnp.tile(norm_g.reshape(1, C_DV), (1, C_HEADS)), s0_bd)
    s5 = sfin.reshape(nb, C_HEADS, C_DK, C_HEADS, C_DV)
    s_fin = jnp.stack([s5[:, h, :, h, :] for h in range(C_HEADS)], axis=1)
    return yc, s_fin


def _out_kernel(x_ref, ya_ref, yb_ref, yc_ref, mod_ref, w_ref, g_ref, b_ref, o_ref):
    y = _dot(ya_ref[...].astype(BF16), w_ref[0:256, :])
    y = y + _dot(yb_ref[...].astype(BF16), w_ref[256:768, :])
    y = y + _dot(yc_ref[...].astype(BF16), w_ref[768:1024, :])
    r = DN_ALPHA * x_ref[...] + mod_ref[0, 2:3, :] * y
    o_ref[...] = _ln(r) * g_ref[...] + b_ref[...]


def out_proj(x, ya, yb, yc, mod, w_out, ln_g, ln_b, t):
    n = x.shape[0]
    tm = min(512, t)
    per = t // tm
    row = lambda i: (i, 0)
    fixed = lambda i: (0, 0)
    return pl.pallas_call(
        _out_kernel,
        grid=(n // tm,),
        in_specs=[
            pl.BlockSpec((tm, D_MODEL), row),
            pl.BlockSpec((tm, A_WIDTH), row),
            pl.BlockSpec((tm, B_WIDTH), row),
            pl.BlockSpec((tm, C_WIDTH), row),
            pl.BlockSpec((1, 6, D_MODEL), lambda i: (i // per, 0, 0)),
            pl.BlockSpec((D_MODEL, D_MODEL), fixed),
            pl.BlockSpec((1, D_MODEL), fixed),
            pl.BlockSpec((1, D_MODEL), fixed),
        ],
        out_specs=pl.BlockSpec((tm, D_MODEL), row),
        out_shape=jax.ShapeDtypeStruct((n, D_MODEL), F32),
        compiler_params=_cparams("parallel"),
        name="out_proj",
    )(x, ya, yb, yc, mod, w_out, ln_g.reshape(1, D_MODEL), ln_b.reshape(1, D_MODEL))


def _extract_top(vals, order, payload, out_s_ref, out_p_ref, base):
    big = jnp.int32(1 << 30)
    for j in range(PEER_TOPK):
        m = jnp.max(vals, axis=0, keepdims=True)
        pos = jnp.min(jnp.where(vals == m, order, big), axis=0, keepdims=True)
        hit = order == pos
        out_s_ref[base + j:base + j + 1, :] = m
        out_p_ref[base + j:base + j + 1, :] = jnp.max(jnp.where(hit, payload, -1), axis=0, keepdims=True)
        vals = jnp.where(hit, -jnp.inf, vals)


def _mod_row(mod_ref, i):
    return mod_ref[0, i:i + 1, :] if len(mod_ref.shape) == 3 else mod_ref[0, i]


def _mod_spec(mod, t, tm, n):
    if tm > t:
        mod = jnp.repeat(mod, t, axis=0).reshape(n // tm, tm, 6, D_MODEL).transpose(0, 2, 1, 3)
        return mod, pl.BlockSpec((1, 6, tm, D_MODEL), lambda i, *_: (i, 0, 0, 0))
    per = t // tm
    return mod, pl.BlockSpec((1, 6, D_MODEL), lambda i, *_: (i // per, 0, 0))


def _peer_a_kernel(x_ref, mod_ref, wq_ref, keys_ref, h2_ref, ids_ref, gates_ref, ts_scr, ti_scr, bs_scr):
    tm = x_ref.shape[0]
    h2 = _ln(x_ref[...]) * (1.0 + _mod_row(mod_ref, 4)) + _mod_row(mod_ref, 3)
    hb = h2.astype(BF16)
    h2_ref[...] = hb
    krow = _iota((PEER_NKEYS, tm), 0)
    for hp in range(2 * PEER_HEADS):
        qhp = _dot(hb, wq_ref[:, hp * LANES:(hp + 1) * LANES]).astype(BF16)
        st = _dot(keys_ref[hp], qhp, NT)
        _extract_top(st, krow, krow, ts_scr, ti_scr, hp * PEER_TOPK)
    r8 = _iota((8, tm), 0)
    r16 = _iota((PEER_TOPK, tm), 0)
    for h in range(PEER_HEADS):
        b0 = 2 * h * PEER_TOPK
        b1 = b0 + PEER_TOPK
        cand = [ts_scr[b0:b0 + 1, :] + ts_scr[b1:b1 + PEER_TOPK, :]]
        ids = [ti_scr[b0:b0 + 1, :] * PEER_NKEYS + ti_scr[b1:b1 + PEER_TOPK, :]]
        order = [r16]
        for a in range(1, PEER_TOPK):
            keep = r8 < (PEER_TOPK // (a + 1))
            cand.append(jnp.where(keep, ts_scr[b0 + a:b0 + a + 1, :] + ts_scr[b1:b1 + 8, :], -jnp.inf))
            ids.append(ti_scr[b0 + a:b0 + a + 1, :] * PEER_NKEYS + ti_scr[b1:b1 + 8, :])
            order.append(a * PEER_TOPK + r8)
        _extract_top(jnp.concatenate(cand, axis=0), jnp.concatenate(order, axis=0), jnp.concatenate(ids, axis=0),
                     bs_scr, ids_ref, h * PEER_TOPK)
        bs = bs_scr[h * PEER_TOPK:(h + 1) * PEER_TOPK, :]
        e = jnp.exp(bs - bs[0:1, :])
        gates_ref[h * PEER_TOPK:(h + 1) * PEER_TOPK, :] = e / jnp.sum(e, axis=0, keepdims=True)


def peer_retrieve(x, mod, wq, keys, t):
    n = x.shape[0]
    tm = min(256, n)
    row = lambda i: (i, 0)
    col = lambda i: (0, i)
    mod, mod_spec = _mod_spec(mod, t, tm, n)
    return pl.pallas_call(
        _peer_a_kernel,
        grid=(n // tm,),
        in_specs=[
            pl.BlockSpec((tm, D_MODEL), row),
            mod_spec,
            pl.BlockSpec((D_MODEL, PEER_HEADS * PEER_DQ), lambda i: (0, 0)),
            pl.BlockSpec((2 * PEER_HEADS, PEER_NKEYS, PEER_DQ // 2), lambda i: (0, 0, 0)),
        ],
        out_specs=[pl.BlockSpec((tm, D_MODEL), row), pl.BlockSpec((PEER_SLOTS, tm), col), pl.BlockSpec((PEER_SLOTS, tm), col)],
        out_shape=[
            jax.ShapeDtypeStruct((n, D_MODEL), BF16),
            jax.ShapeDtypeStruct((PEER_SLOTS, n), jnp.int32),
            jax.ShapeDtypeStruct((PEER_SLOTS, n), F32),
        ],
        scratch_shapes=[
            pltpu.VMEM((2 * PEER_HEADS * PEER_TOPK, tm), F32),
            pltpu.VMEM((2 * PEER_HEADS * PEER_TOPK, tm), jnp.int32),
            pltpu.VMEM((PEER_SLOTS, tm), F32),
        ],
        compiler_params=_cparams("parallel"),
        name="peer_retrieve",
    )(x, mod, wq, keys)


PEER_CI = 16


def _peer_b_kernel(h2_ref, ids_ref, gates_ref, ut_ref, v_ref, x_ref, mod_ref, g_ref, b_ref, o_ref,
                   gmat_scr, acc_scr, *, tm):
    c = pl.program_id(1)

    @pl.when(c == 0)
    def _():
        sub = _iota((PEER_NKEYS, PEER_SLOTS), 0)

        def tok(n, carry):
            idrow = ids_ref[pl.ds(n, 1), :]
            grow = gates_ref[pl.ds(n, 1), :]
            a = jnp.where(sub == (idrow >> 7), 1.0, 0.0).astype(BF16)
            b = jnp.where(sub == (idrow & (PEER_NKEYS - 1)), grow, 0.0).astype(BF16)
            gmat_scr[pl.ds(n, PEER_NKEYS, stride=tm), :] = _dot(a, b, NT)
            return carry

        lax.fori_loop(0, tm, tok, 0)
        acc_scr[...] = jnp.zeros_like(acc_scr)

    act = _dot(h2_ref[...], ut_ref[...])
    ws = []
    for j in range(PEER_CI):
        start = pl.multiple_of((c * PEER_CI + j) * tm, tm)
        gj = gmat_scr[pl.ds(start, tm), :]
        ws.append((gj * jax.nn.gelu(act[:, j * LANES:(j + 1) * LANES])).astype(BF16))
    acc_scr[...] += _dot(jnp.concatenate(ws, axis=1), v_ref[...])

    @pl.when(c == pl.num_programs(1) - 1)
    def _():
        r = DN_ALPHA * x_ref[...] + _mod_row(mod_ref, 5) * acc_scr[...]
        o_ref[...] = _ln(r) * g_ref[...] + b_ref[...]


def peer_mix(h2, ids, gates, ut, vv, x, mod, ln_g, ln_b, t):
    n = x.shape[0]
    tm = min(256, n)
    ec = PEER_CI * PEER_NKEYS
    row = lambda i, c: (i, 0)
    fixed = lambda i, c: (0, 0)
    mod, mod_spec = _mod_spec(mod, t, tm, n)
    return pl.pallas_call(
        functools.partial(_peer_b_kernel, tm=tm),
        grid=(n // tm, PEER_NKEYS // PEER_CI),
        in_specs=[
            pl.BlockSpec((tm, D_MODEL), row),
            pl.BlockSpec((tm, PEER_SLOTS), row),
            pl.BlockSpec((tm, PEER_SLOTS), row),
            pl.BlockSpec((D_MODEL, ec), lambda i, c: (0, c)),
            pl.BlockSpec((ec, D_MODEL), lambda i, c: (c, 0)),
            pl.BlockSpec((tm, D_MODEL), row),
            mod_spec,
            pl.BlockSpec((1, D_MODEL), fixed),
            pl.BlockSpec((1, D_MODEL), fixed),
        ],
        out_specs=pl.BlockSpec((tm, D_MODEL), row),
        out_shape=jax.ShapeDtypeStruct((n, D_MODEL), F32),
        scratch_shapes=[pltpu.VMEM((PEER_NKEYS * tm, PEER_NKEYS), F32), pltpu.VMEM((tm, D_MODEL), F32)],
        compiler_params=_cparams("parallel", "arbitrary"),
        name="peer_mix",
    )(h2, ids, gates, ut, vv, x, mod, ln_g.reshape(1, D_MODEL), ln_b.reshape(1, D_MODEL))


def _run_group(x3, mod_g, past_k, past_v, gdn_state, conv_state, wts):
    nb, t, _ = x3.shape
    n = nb * t
    cached = past_k is not None
    x = x3.reshape(n, D_MODEL)
    tp = -(-t // CHUNK) * CHUNK
    new_k, new_v, new_s, new_conv, new_sgu_v = [], [], [], [], []
    for l in range(DEPTH):
        mod = mod_g[l]
        pa, q, k, v, pc, pab = in_proj(x, mod, wts["w_main"][l], wts["w_ab"][l], t)
        ya, va = sgu(pa, wts["sgu_ln_g"][l], wts["sgu_ln_b"][l], wts["sgu_w"][l], wts["sgu_b"][l], t)
        lam_init = 0.8 - 0.6 * math.exp(-0.3 * l)
        lam = (jnp.exp(jnp.sum(wts["lam_q1"][l] * wts["lam_k1"][l]))
               - jnp.exp(jnp.sum(wts["lam_q2"][l] * wts["lam_k2"][l])) + lam_init)
        q3, k3, v3 = (a.reshape(nb, t, B_WIDTH) for a in (q, k, v))
        if cached:
            p_len = past_k.shape[2]
            tk = 256
            klen = p_len + t
            padk = -klen % tk
            kk = jnp.concatenate([past_k[l].reshape(nb, p_len, B_QK), k3, jnp.zeros((nb, padk, B_QK), F32)], axis=1)
            vv = jnp.concatenate([past_v[l].reshape(nb, p_len, B_WIDTH), v3, jnp.zeros((nb, padk, B_WIDTH), F32)], axis=1)
            yb = diff_attn(q3, kk, vv, lam, wts["diff_norm_g"][l], 1.0 - lam_init, q_off=p_len, klen=klen, tq=t, tk=tk)
        else:
            yb = diff_attn(q3, k3, v3, lam, wts["diff_norm_g"][l], 1.0 - lam_init, q_off=0, klen=t, tq=128, tk=256)
        pc3 = pc.reshape(nb, t, 1024)
        pab3 = pab.reshape(nb, t, LANES)
        if tp != t:
            pc3p = jnp.pad(pc3, ((0, 0), (0, tp - t), (0, 0)))
            pab3p = jnp.pad(pab3, ((0, 0), (0, tp - t), (0, 0)))
        else:
            pc3p, pab3p = pc3, pab3
        cst = conv_state[l] if cached else jnp.zeros((nb, CONV_W - 1, C_QKV), F32)
        s0 = gdn_state[l] if cached else jnp.zeros((nb, C_HEADS, C_DK, C_DV), F32)
        yc, s_fin = gdn(pc3p, pab3p, cst, s0, wts["conv_w"][l], wts["gdn_a_log"][l], wts["gdn_dt_bias"][l],
                        wts["gdn_norm_g"][l], t)
        yc = yc[:, :t].reshape(n, C_WIDTH)
        x = out_proj(x, ya, yb.reshape(n, B_WIDTH), yc, mod, wts["w_out"][l], wts["ln1_g"][l], wts["ln1_b"][l], t)
        h2, ids, gates = peer_retrieve(x, mod, wts["peer_wq"][l], wts["peer_keys"][l], t)
        x = peer_mix(h2, ids.T, gates.T, wts["expert_ut"][l], wts["expert_v"][l], x, mod,
                     wts["ln2_g"][l], wts["ln2_b"][l], t)
        new_k.append(k3.reshape(nb, t, B_HEADS, 2 * B_DK))
        new_v.append(v3.reshape(nb, t, B_HEADS, B_DV))
        new_s.append(s_fin)
        xp_tail = jnp.concatenate([cst, pc3[:, :, :C_QKV]], axis=1)[:, -(CONV_W - 1):] if t < CONV_W - 1 \
            else pc3[:, t - (CONV_W - 1):, :C_QKV]
        new_conv.append(xp_tail)
        new_sgu_v.append(va.reshape(nb, t, A_WIDTH))
    return (x.reshape(nb, t, D_MODEL), jnp.stack(new_k), jnp.stack(new_v), jnp.stack(new_s), jnp.stack(new_conv),
            jnp.stack(new_sgu_v))


def kernel(x_prompt, x_sample, cache_k, cache_v, state_gdn, state_conv, c_prompt, c_sample,
           w_ada, b_ada, w_in, sgu_ln_g, sgu_ln_b, sgu_w, sgu_b,
           lam_q1, lam_k1, lam_q2, lam_k2, diff_norm_g,
           conv_w, gdn_a_log, gdn_dt_bias, gdn_norm_g,
           w_out, ln1_g, ln1_b, peer_wq, peer_keys, expert_u, expert_v, ln2_g, ln2_b):
    nbp = x_prompt.shape[0]
    nbs = x_sample.shape[0]
    wts = dict(
        w_main=w_in[:, :, :P_MAIN].astype(BF16),
        w_ab=jnp.pad(w_in[:, :, P_MAIN:], ((0, 0), (0, 0), (0, LANES - (P_IN - P_MAIN)))).astype(BF16),
        sgu_ln_g=sgu_ln_g, sgu_ln_b=sgu_ln_b, sgu_w=sgu_w, sgu_b=sgu_b,
        lam_q1=lam_q1, lam_k1=lam_k1, lam_q2=lam_q2, lam_k2=lam_k2, diff_norm_g=diff_norm_g,
        conv_w=conv_w, gdn_a_log=gdn_a_log, gdn_dt_bias=gdn_dt_bias, gdn_norm_g=gdn_norm_g,
        w_out=w_out.astype(BF16), ln1_g=ln1_g, ln1_b=ln1_b,
        peer_wq=peer_wq.astype(BF16),
        peer_keys=peer_keys.reshape(DEPTH, 2 * PEER_HEADS, PEER_NKEYS, PEER_DQ // 2).astype(BF16),
        expert_ut=jnp.swapaxes(expert_u, 1, 2).astype(BF16),
        expert_v=expert_v.astype(BF16),
        ln2_g=ln2_g, ln2_b=ln2_b,
    )
    mod_all = ada_mod(jnp.concatenate([c_prompt, c_sample], axis=0), w_ada, b_ada)
    mod_all = mod_all.reshape(DEPTH, nbp + nbs, 6, D_MODEL)
    y_p, k_p, v_p, s_p, conv_p, _ = _run_group(x_prompt, mod_all[:, :nbp], None, None, None, None, wts)
    y_s, k_s, v_s, s_s, conv_s, sgu_s = _run_group(x_sample, mod_all[:, nbp:], cache_k, cache_v, state_gdn, state_conv, wts)
    return (y_p, y_s, k_p, v_p, s_p, conv_p, k_s, v_s, s_s, conv_s, sgu_s)
```

```python
import functools
import math

import jax
import jax.numpy as jnp
from jax import lax
from jax.experimental import pallas as pl
from jax.experimental.pallas import tpu as pltpu

F32 = jnp.float32
BF16 = jnp.bfloat16

D_MODEL = 1024
DEPTH = 4
CHUNK = 64
A_GROUPS = 4
A_GDIM = 64
A_WIDTH = 256
SGU_LEN = 128
B_HEADS = 4
B_DK = 64
B_DV = 128
B_WIDTH = 512
B_QK = 512
C_HEADS = 4
C_DK = 64
C_DV = 64
C_WIDTH = 256
CONV_W = 4
C_QKV = 768
P_MAIN = 3072
P_IN = 3080
PEER_HEADS = 8
PEER_NKEYS = 128
PEER_DQ = 256
PEER_TOPK = 16
PEER_SLOTS = PEER_HEADS * PEER_TOPK
DN_ALPHA = (2 * DEPTH) ** 0.25
LN_EPS = 1e-5
NEG = -1e30

LANES = 128
VMEM_LIMIT = 56 * 1024 * 1024

NN = (((1,), (0,)), ((), ()))
NT = (((1,), (1,)), ((), ()))


def _cparams(*sem):
    return pltpu.CompilerParams(dimension_semantics=sem, vmem_limit_bytes=VMEM_LIMIT)


def _dot(a, b, dims=NN):
    return lax.dot_general(a, b, dims, preferred_element_type=F32)


def _split(x, n):
    parts = []
    r = x
    for _ in range(n - 1):
        p = r.astype(BF16)
        parts.append(p)
        r = r - p.astype(F32)
    parts.append(r.astype(BF16))
    return parts


def _dot3(a, b, dims=NN):
    ah, al = _split(a, 2)
    bh, bl = _split(b, 2)
    return _dot(ah, bh, dims) + (_dot(ah, bl, dims) + _dot(al, bh, dims))


def _dot_exact_rhs(x, m_bf16, n=3, dims=NN):
    parts = _split(x, n)
    acc = _dot(parts[-1], m_bf16, dims)
    for p in parts[-2::-1]:
        acc = acc + _dot(p, m_bf16, dims)
    return acc


def _dot_exact_lhs(m_bf16, x, n=3, dims=NN):
    parts = _split(x, n)
    acc = _dot(m_bf16, parts[-1], dims)
    for p in parts[-2::-1]:
        acc = acc + _dot(m_bf16, p, dims)
    return acc


def _ln(x):
    mu = jnp.mean(x, axis=-1, keepdims=True)
    d = x - mu
    var = jnp.mean(d * d, axis=-1, keepdims=True)
    return d * lax.rsqrt(var + LN_EPS)


def _iota(shape, dim):
    return lax.broadcasted_iota(jnp.int32, shape, dim)


def _ada_kernel(c_ref, w_ref, b_ref, o_ref):
    c = c_ref[...]
    s = c * jax.nn.sigmoid(c)
    o_ref[0] = _dot3(s, w_ref[0]) + b_ref[0]


def ada_mod(c_all, w_ada, b_ada):
    nbt = c_all.shape[0]
    tn = 1536
    return pl.pallas_call(
        _ada_kernel,
        grid=(DEPTH, 6 * D_MODEL // tn),
        in_specs=[
            pl.BlockSpec((nbt, D_MODEL), lambda l, j: (0, 0)),
            pl.BlockSpec((1, D_MODEL, tn), lambda l, j: (l, 0, j)),
            pl.BlockSpec((1, 1, tn), lambda l, j: (l, 0, j)),
        ],
        out_specs=pl.BlockSpec((1, nbt, tn), lambda l, j: (l, 0, j)),
        out_shape=jax.ShapeDtypeStruct((DEPTH, nbt, 6 * D_MODEL), F32),
        compiler_params=_cparams("parallel", "parallel"),
        name="ada_mod",
    )(c_all, w_ada, b_ada.reshape(DEPTH, 1, 6 * D_MODEL))


def _in_kernel(x_ref, mod_ref, w_ref, wab_ref, pa_ref, q_ref, k_ref, v_ref, pc_ref, pab_ref):
    h = _ln(x_ref[...]) * (1.0 + mod_ref[0, 1:2, :]) + mod_ref[0, 0:1, :]
    hb = h.astype(BF16)
    pa_ref[...] = _dot(hb, w_ref[:, 0:512])
    q_ref[...] = _dot(hb, w_ref[:, 512:1024])
    k_ref[...] = _dot(hb, w_ref[:, 1024:1536])
    v_ref[...] = _dot(hb, w_ref[:, 1536:2048])
    pc_ref[...] = _dot(hb, w_ref[:, 2048:3072])
    pab_ref[...] = _dot(hb, wab_ref[...])


def in_proj(x, mod, w_main, w_ab, t):
    n = x.shape[0]
    tm = min(512, t)
    per = t // tm
    row = lambda i: (i, 0)
    fixed = lambda i: (0, 0)
    sds = lambda w: jax.ShapeDtypeStruct((n, w), F32)
    return pl.pallas_call(
        _in_kernel,
        grid=(n // tm,),
        in_specs=[
            pl.BlockSpec((tm, D_MODEL), row),
            pl.BlockSpec((1, 6, D_MODEL), lambda i: (i // per, 0, 0)),
            pl.BlockSpec((D_MODEL, P_MAIN), fixed),
            pl.BlockSpec((D_MODEL, LANES), fixed),
        ],
        out_specs=[pl.BlockSpec((tm, 512), row)] * 4 + [pl.BlockSpec((tm, 1024), row), pl.BlockSpec((tm, LANES), row)],
        out_shape=[sds(512)] * 4 + [sds(1024), sds(LANES)],
        compiler_params=_cparams("parallel"),
        name="in_proj",
    )(x, mod, w_main, w_ab)


def _group_ones(width, gdim):
    r = _iota((width, width), 0) // gdim
    c = _iota((width, width), 1) // gdim
    return jnp.where(r == c, 1.0, 0.0).astype(BF16)


def _sgu_kernel(pa_ref, g_ref, b_ref, w_ref, bs_ref, ya_ref, va_ref, *, blk, nblk):
    pa = pa_ref[...]
    u = jax.nn.gelu(pa[:, :A_WIDTH])
    v = jax.nn.gelu(pa[:, A_WIDTH:])
    ones = _group_ones(A_WIDTH, A_GDIM)
    mean = _dot_exact_rhs(v, ones) * (1.0 / A_GDIM)
    d = v - mean
    var = _dot_exact_rhs(d * d, ones) * (1.0 / A_GDIM)
    vn = d * lax.rsqrt(var + LN_EPS) * g_ref[...] + b_ref[...]
    va_ref[...] = vn
    ri = _iota((blk, blk), 0) // CHUNK
    ci = _iota((blk, blk), 1) // CHUNK
    lane_group = _iota((blk, A_WIDTH), 1) // A_GDIM
    wm = [jnp.where(ci <= ri, w_ref[g], 0.0).astype(BF16) for g in range(A_GROUPS)]
    for r in range(nblk):
        vb = vn[r * blk:(r + 1) * blk, :].astype(BF16)
        s = jnp.zeros((blk, A_WIDTH), F32)
        for g in range(A_GROUPS):
            s = jnp.where(lane_group == g, _dot(wm[g], vb), s)
        ya_ref[r * blk:(r + 1) * blk, :] = u[r * blk:(r + 1) * blk, :] * (s + bs_ref[...])


def sgu(pa, ln_g, ln_b, w_s, b_s, t):
    n = pa.shape[0]
    blk = min(SGU_LEN, t)
    assert t % blk == 0
    tm = min(512, n)
    w = w_s[:, :blk, :blk]
    bs = jnp.repeat(b_s.T[:blk], A_GDIM, axis=1)
    row = lambda i: (i, 0)
    return pl.pallas_call(
        functools.partial(_sgu_kernel, blk=blk, nblk=tm // blk),
        grid=(n // tm,),
        in_specs=[
            pl.BlockSpec((tm, 2 * A_WIDTH), row),
            pl.BlockSpec((1, A_WIDTH), lambda i: (0, 0)),
            pl.BlockSpec((1, A_WIDTH), lambda i: (0, 0)),
            pl.BlockSpec((A_GROUPS, blk, blk), lambda i: (0, 0, 0)),
            pl.BlockSpec((blk, A_WIDTH), lambda i: (0, 0)),
        ],
        out_specs=[pl.BlockSpec((tm, A_WIDTH), row)] * 2,
        out_shape=[jax.ShapeDtypeStruct((n, A_WIDTH), F32)] * 2,
        compiler_params=_cparams("parallel"),
        name="sgu",
    )(pa, ln_g.reshape(1, A_WIDTH), ln_b.reshape(1, A_WIDTH), w, bs)


def _attn_kernel(lam_ref, slope_ref, q_ref, k_ref, v_ref, g_ref, o_ref, *, tq, tk, q_off, klen, out_scale):
    h = pl.program_id(1)
    qi = pl.program_id(2)
    rows = 2 * tq
    log2e = 1.4426950408889634
    q = q_ref[0] * (B_DK ** -0.5 * log2e)
    lane = _iota((tq, B_DV), 1)
    qq = jnp.concatenate([jnp.where(lane < B_DK, q, 0.0), jnp.where(lane >= B_DK, q, 0.0)], axis=0).astype(BF16)
    q_lo = q_off + qi * tq
    qpos = q_lo + (_iota((rows, tk), 0) & (tq - 1))
    col = _iota((rows, tk), 1)
    colf = _iota((1, tk), 1).astype(F32)
    slope2 = slope_ref[h] * log2e
    limit = jnp.minimum(klen, ((q_lo + tq - 1) // CHUNK + 1) * CHUNK)
    nkb = (limit + tk - 1) // tk
    nfull = jnp.minimum(q_lo, klen) // tk

    def scores(kb):
        start = pl.multiple_of(kb * tk, tk)
        return _dot(qq, k_ref[0, pl.ds(start, tk), :].astype(BF16), NT)

    def update(kb, s, m, l, acc):
        start = pl.multiple_of(kb * tk, tk)
        vblk = v_ref[0, pl.ds(start, tk), :].astype(BF16)
        m_new = jnp.maximum(m, jnp.max(s, axis=1, keepdims=True))
        alpha = jnp.exp2(m - m_new)
        p = jnp.exp2(s - m_new)
        l = alpha * l + jnp.sum(p, axis=1, keepdims=True)
        acc = alpha * acc + _dot(p.astype(BF16), vblk)
        return m_new, l, acc

    def masked(kb, s):
        kpos = kb * tk + col
        s = s + slope2 * (qpos - jnp.abs(qpos - kpos)).astype(F32)
        kchunk = jnp.where(kpos < klen, kpos // CHUNK, jnp.int32(1 << 30))
        return jnp.where(kchunk <= (qpos // CHUNK), s, NEG)

    def full_body(kb, carry):
        m, l, acc, s = carry
        s_next = scores(kb + 1)
        s = s + slope2 * ((kb * tk).astype(F32) + colf)
        return update(kb, s, m, l, acc) + (s_next,)

    def tail_body(kb, carry):
        return update(kb, masked(kb, scores(kb)), *carry)

    init = (jnp.full((rows, 1), NEG, F32), jnp.zeros((rows, 1), F32), jnp.zeros((rows, B_DV), F32), scores(0))
    m, l, acc, s = lax.fori_loop(0, nfull, full_body, init)
    carry = update(nfull, masked(nfull, s), m, l, acc)
    m, l, acc = lax.fori_loop(nfull + 1, nkb, tail_body, carry)
    o = acc / l
    ob = o[:tq] - lam_ref[0] * o[tq:]
    ms = jnp.mean(ob * ob, axis=-1, keepdims=True)
    o_ref[0] = ob * lax.rsqrt(ms + LN_EPS) * g_ref[...] * out_scale


def diff_attn(q, k, v, lam, norm_g, out_scale, *, q_off, klen, tq, tk):
    nb, t, _ = q.shape
    tkp = k.shape[1]
    assert klen >= q_off + t and tkp % tk == 0 and tkp >= klen and t % tq == 0
    slopes =jnp.exp2(-(8.0 / B_HEADS) * jnp.arange(1, B_HEADS + 1, dtype=F32))
    smem = pl.BlockSpec(memory_space=pltpu.SMEM)
    return pl.pallas_call(
        functools.partial(_attn_kernel, tq=tq, tk=tk, q_off=q_off, klen=klen, out_scale=out_scale),
        grid=(nb, B_HEADS, t // tq),
        in_specs=[
            smem, smem,
            pl.BlockSpec((1, tq, B_DV), lambda b, h, i: (b, i, h)),
            pl.BlockSpec((1, tkp, B_DV), lambda b, h, i: (b, 0, h)),
            pl.BlockSpec((1, tkp, B_DV), lambda b, h, i: (b, 0, h)),
            pl.BlockSpec((1, B_DV), lambda b, h, i: (0, 0)),
        ],
        out_specs=pl.BlockSpec((1, tq, B_DV), lambda b, h, i: (b, i, h)),
        out_shape=jax.ShapeDtypeStruct((nb, t, B_WIDTH), F32),
        compiler_params=_cparams("parallel", "parallel", "arbitrary"),
        name="diff_attn",
    )(lam.reshape(1), slopes, q, k, v, norm_g.reshape(1, B_DV))


GDN_ROWS = 4


def _gdn_kernel(pc_ref, pab_ref, cst_ref, cw_ref, alog_ref, dtb_ref, ng_ref, s0_ref,
                yc_ref, sfin_ref, prev_scr, *, t_valid, bb):
    c = pl.program_id(1)

    @pl.when(c == 0)
    def _():
        prev_scr[...] = cst_ref[...]
        sfin_ref[...] = s0_ref[...]

    rows = [_gdn_chunk(bi, c, pc_ref, pab_ref, cw_ref, alog_ref, dtb_ref, ng_ref, yc_ref, sfin_ref, prev_scr, t_valid)
            for bi in range(bb)]
    while rows:
        rows = [r for r in rows if next(r, True) is None]


def _gdn_chunk(bi, c, pc_ref, pab_ref, cw_ref, alog_ref, dtb_ref, ng_ref, yc_ref, sfin_ref, prev_scr, t_valid):
    W = C_WIDTH
    qkv = pc_ref[bi, :, 0:C_QKV]
    z = pc_ref[bi, :, C_QKV:C_QKV + W]
    ext = jnp.concatenate([prev_scr[bi], qkv], axis=0)
    prev_scr[bi] = qkv[CHUNK - 8:, :]
    conv = ext[5:5 + CHUNK] * cw_ref[0:1, :]
    for j in range(1, CONV_W):
        conv = conv + ext[5 + j:5 + j + CHUNK] * cw_ref[j:j + 1, :]
    conv = conv * jax.nn.sigmoid(conv)
    q, k, v = conv[:, 0:W], conv[:, W:2 * W], conv[:, 2 * W:3 * W]

    head_ones = _group_ones(W, C_DK)
    qss = _dot_exact_rhs(q * q, head_ones)
    kss = _dot_exact_rhs(k * k, head_ones)
    pab = pab_ref[bi]
    g_cols = -jnp.exp(alog_ref[...]) * jax.nn.softplus(pab + dtb_ref[...])
    b_cols = jax.nn.sigmoid(pab)
    er = _iota((LANES, W), 0)
    eh = _iota((LANES, W), 1) // C_DK
    g = _dot_exact_rhs(g_cols, jnp.where(er == eh, 1.0, 0.0).astype(BF16))
    beta = _dot_exact_rhs(b_cols, jnp.where(er == eh + C_HEADS, 1.0, 0.0).astype(BF16))
    yield
    q = q * lax.rsqrt(qss + 1e-6) * (C_DK ** -0.5)
    k = k * lax.rsqrt(kss + 1e-6)
    if t_valid % CHUNK:
        valid = (c * CHUNK + _iota((CHUNK, W), 0)) < t_valid
        q, k, v = (jnp.where(valid, a, 0.0) for a in (q, k, v))
        g = jnp.where(valid, g, 0.0)
        beta = jnp.where(valid, beta, 0.0)

    tri = jnp.where(_iota((CHUNK, CHUNK), 0) >= _iota((CHUNK, CHUNK), 1), 1.0, 0.0).astype(BF16)
    gc = _dot_exact_lhs(tri, g)
    yield
    eg = jnp.exp(gc)
    glast = gc[CHUNK - 1:CHUNK, :]
    kdec = k * jnp.exp(glast - gc)
    qg = q * eg
    kb = k * beta
    kbe = kb * eg
    vb = v * beta

    R = C_HEADS * CHUNK
    same = (_iota((R, W), 0) // CHUNK) == (_iota((R, W), 1) // C_DK)
    ii = _iota((CHUNK, W), 0)
    jj = _iota((CHUNK, W), 1) % CHUNK

    def bd(a):
        return jnp.where(same, jnp.concatenate([a] * C_HEADS, axis=0), 0.0)

    def split_bd(a):
        hi = a.astype(BF16)
        lo = (a - hi.astype(F32)).astype(BF16)
        return hi, lo, bd(hi.astype(F32)).astype(BF16), bd(lo.astype(F32)).astype(BF16)

    def mm3(a_hi, a_lo, b_hi, b_lo):
        return _dot(a_hi, b_hi) + (_dot(a_hi, b_lo) + _dot(a_lo, b_hi))

    kx = bd(k).astype(BF16)
    kk = _dot(kb.astype(BF16), kx, NT)
    qk = _dot(q.astype(BF16), kx, NT)
    pick = jnp.where(jj == 0, 1.0, 0.0).astype(BF16)
    g_col = _dot_exact_lhs(pick, bd(gc), dims=NT)
    yield
    lower = ii >= jj
    gamma = jnp.where(lower, jnp.exp(jnp.where(lower, gc - g_col, 0.0)), 0.0)
    nm = jnp.where(ii > jj, -(kk * gamma), 0.0)
    aqk = qk * gamma

    tinv = jnp.where(ii == jj, 1.0, 0.0) + nm
    p_hi, p_lo, pb_hi, pb_lo = split_bd(nm)
    for step in range(5):
        pw = mm3(p_hi, p_lo, pb_hi, pb_lo)
        yield
        p_hi, p_lo, pb_hi, pb_lo = split_bd(pw)
        t_hi, t_lo = _split(tinv, 2)
        tinv = tinv + mm3(t_hi, t_lo, pb_hi, pb_lo)
    _, _, vb_hi, vb_lo = split_bd(vb)
    _, _, ke_hi, ke_lo = split_bd(kbe)
    yield
    t_hi, t_lo = _split(tinv, 2)
    u = mm3(t_hi, t_lo, vb_hi, vb_lo)
    w = mm3(t_hi, t_lo, ke_hi, ke_lo)
    s = sfin_ref[bi]
    sb = s.astype(BF16)
    oq = _dot(qg.astype(BF16), sb)
    kdt = kdec.T.astype(BF16)
    yield
    v_new = u - _dot(w.astype(BF16), sb)
    yield
    vnb = v_new.astype(BF16)
    o = oq + _dot(aqk.astype(BF16), bd(v_new).astype(BF16))
    sfin_ref[bi] = s * jnp.exp(glast) + jnp.where(same, _dot(kdt, vnb), 0.0)
    yield
    ms = _dot_exact_rhs(o * o, head_ones) * (1.0 / C_DV)
    yield
    yc_ref[bi] = o * lax.rsqrt(ms + LN_EPS) * ng_ref[...] * (z * jax.nn.sigmoid(z))


def gdn(pc, pab, conv_state, s0, conv_w, a_log, dt_bias, norm_g, t_valid):
    nb, tp, _ = pc.shape
    cst = jnp.pad(conv_state, ((0, 0), (8 - (CONV_W - 1), 0), (0, 0)))
    cw = jnp.pad(conv_w, ((0, 8 - CONV_W), (0, 0)))
    pad_row = lambda a: jnp.pad(a.reshape(1, C_HEADS), ((0, 0), (0, LANES - C_HEADS)))
    eye = jnp.eye(C_HEADS, dtype=F32)
    s0_bd = jnp.einsum('bhde,hg->bhdge', s0, eye).reshape(nb, C_HEADS * C_DK, C_HEADS * C_DV)
    full = lambda b, c: (0, 0)
    bb = math.gcd(nb, GDN_ROWS)
    yc, sfin = pl.pallas_call(
        functools.partial(_gdn_kernel, t_valid=t_valid, bb=bb),
        grid=(nb // bb, tp // CHUNK),
        in_specs=[
            pl.BlockSpec((bb, CHUNK, 1024), lambda b, c: (b, c, 0)),
            pl.BlockSpec((bb, CHUNK, LANES), lambda b, c: (b, c, 0)),
            pl.BlockSpec((bb, 8, C_QKV), lambda b, c: (b, 0, 0)),
            pl.BlockSpec((8, C_QKV), full),
            pl.BlockSpec((1, LANES), full),
            pl.BlockSpec((1, LANES), full),
            pl.BlockSpec((1, C_WIDTH), full),
            pl.BlockSpec((bb, C_WIDTH, C_WIDTH), lambda b, c: (b, 0, 0)),
        ],
        out_specs=[
            pl.BlockSpec((bb, CHUNK, C_WIDTH), lambda b, c: (b, c, 0)),
            pl.BlockSpec((bb, C_WIDTH, C_WIDTH), lambda b, c: (b, 0, 0)),
        ],
        out_shape=[
            jax.ShapeDtypeStruct((nb, tp, C_WIDTH), F32),
            jax.ShapeDtypeStruct((nb, C_WIDTH, C_WIDTH), F32),
        ],
        scratch_shapes=[pltpu.VMEM((bb, 8, C_QKV), F32)],
        compiler_params=_cparams("parallel", "arbitrary"),
        name="gdn",
    )(pc, pab, cst, cw, pad_row(a_log), pad_row(dt_bias), jnp.tile(norm_g.reshape(1, C_DV), (1, C_HEADS)), s0_bd)
    s5 = sfin.reshape(nb, C_HEADS, C_DK, C_HEADS, C_DV)
    s_fin = jnp.stack([s5[:, h, :, h, :] for h in range(C_HEADS)], axis=1)
    return yc, s_fin


def _out_kernel(x_ref, ya_ref, yb_ref, yc_ref, mod_ref, w_ref, g_ref, b_ref, o_ref):
    y = _dot(ya_ref[...].astype(BF16), w_ref[0:256, :])
    y = y + _dot(yb_ref[...].astype(BF16), w_ref[256:768, :])
    y = y + _dot(yc_ref[...].astype(BF16), w_ref[768:1024, :])
    r = DN_ALPHA * x_ref[...] + mod_ref[0, 2:3, :] * y
    o_ref[...] = _ln(r) * g_ref[...] + b_ref[...]


def out_proj(x, ya, yb, yc, mod, w_out, ln_g, ln_b, t):
    n = x.shape[0]
    tm = min(512, t)
    per = t // tm
    row = lambda i: (i, 0)
    fixed = lambda i: (0, 0)
    return pl.pallas_call(
        _out_kernel,
        grid=(n // tm,),
        in_specs=[
            pl.BlockSpec((tm, D_MODEL), row),
            pl.BlockSpec((tm, A_WIDTH), row),
            pl.BlockSpec((tm, B_WIDTH), row),
            pl.BlockSpec((tm, C_WIDTH), row),
            pl.BlockSpec((1, 6, D_MODEL), lambda i: (i // per, 0, 0)),
            pl.BlockSpec((D_MODEL, D_MODEL), fixed),
            pl.BlockSpec((1, D_MODEL), fixed),
            pl.BlockSpec((1, D_MODEL), fixed),
        ],
        out_specs=pl.BlockSpec((tm, D_MODEL), row),
        out_shape=jax.ShapeDtypeStruct((n, D_MODEL), F32),
        compiler_params=_cparams("parallel"),
        name="out_proj",
    )(x, ya, yb, yc, mod, w_out, ln_g.reshape(1, D_MODEL), ln_b.reshape(1, D_MODEL))


def _extract_top(vals, order, payload, out_s_ref, out_p_ref, base):
    big = jnp.int32(1 << 30)
    for j in range(PEER_TOPK):
        m = jnp.max(vals, axis=0, keepdims=True)
        tied = jnp.where(vals == m, order, big)
        pos = jnp.min(tied, axis=0, keepdims=True)
        out_s_ref[base + j:base + j + 1, :] = m
        if payload is None:
            out_p_ref[base + j:base + j + 1, :] = pos
        else:
            out_p_ref[base + j:base + j + 1, :] = jnp.max(jnp.where(tied == pos, payload, -1), axis=0, keepdims=True)
        vals = jnp.where(tied == pos, -jnp.inf, vals)


def _mod_row(mod_ref, i):
    return mod_ref[0, i:i + 1, :] if len(mod_ref.shape) == 3 else mod_ref[0, i]


def _mod_spec(mod, t, tm, n):
    if tm > t:
        mod = jnp.repeat(mod, t, axis=0).reshape(n // tm, tm, 6, D_MODEL).transpose(0, 2, 1, 3)
        return mod, pl.BlockSpec((1, 6, tm, D_MODEL), lambda i, *_: (i, 0, 0, 0))
    per = t // tm
    return mod, pl.BlockSpec((1, 6, D_MODEL), lambda i, *_: (i // per, 0, 0))


def _peer_a_kernel(x_ref, mod_ref, wq_ref, keys_ref, h2_ref, ids_ref, gates_ref, ts_scr, ti_scr, bs_scr):
    tm = x_ref.shape[0]
    h2 = _ln(x_ref[...]) * (1.0 + _mod_row(mod_ref, 4)) + _mod_row(mod_ref, 3)
    hb = h2.astype(BF16)
    h2_ref[...] = hb
    krow = _iota((PEER_NKEYS, tm), 0)
    for hp in range(2 * PEER_HEADS):
        qhp = _dot(hb, wq_ref[:, hp * LANES:(hp + 1) * LANES]).astype(BF16)
        st = _dot(keys_ref[hp], qhp, NT)
        _extract_top(st, krow, None, ts_scr, ti_scr, hp * PEER_TOPK)
    r8 = _iota((8, tm), 0)
    for h in range(PEER_HEADS):
        b0 = 2 * h * PEER_TOPK
        b1 = b0 + PEER_TOPK
        cand, ids, order = [], [], []

        def add(sa, sb, rows_a, lo, hi):
            va = ts_scr[b0 + sa[0]:b0 + sa[0] + sa[1], :]
            vb = ts_scr[b1 + sb[0]:b1 + sb[0] + sb[1], :]
            ia = ti_scr[b0 + sa[0]:b0 + sa[0] + sa[1], :]
            ib = ti_scr[b1 + sb[0]:b1 + sb[0] + sb[1], :]
            s = va + vb
            if (lo, hi) != (0, 8):
                s = jnp.where(r8 >= lo, jnp.where(r8 < hi, s, -jnp.inf), -jnp.inf)
            cand.append(s)
            ids.append(ia * PEER_NKEYS + ib)
            order.append((sa[0] + r8) * PEER_TOPK + sb[0] if rows_a else sa[0] * PEER_TOPK + sb[0] + r8)

        add((0, 1), (0, 8), False, 0, 8)
        add((0, 1), (8, 8), False, 0, 8)
        add((1, 1), (0, 8), False, 0, 8)
        add((8, 8), (0, 1), True, 0, 8)
        for b in range(5):
            add((0, 8), (b, 1), True, 2, PEER_TOPK // (b + 1))
        _extract_top(jnp.concatenate(cand, axis=0), jnp.concatenate(order, axis=0), jnp.concatenate(ids, axis=0),
                     bs_scr, ids_ref, h * PEER_TOPK)
        bs = bs_scr[h * PEER_TOPK:(h + 1) * PEER_TOPK, :]
        e = jnp.exp(bs - bs[0:1, :])
        gates_ref[h * PEER_TOPK:(h + 1) * PEER_TOPK, :] = e / jnp.sum(e, axis=0, keepdims=True)


def peer_retrieve(x, mod, wq, keys, t):
    n = x.shape[0]
    tm = min(256, n)
    row = lambda i: (i, 0)
    col = lambda i: (0, i)
    mod, mod_spec = _mod_spec(mod, t, tm, n)
    return pl.pallas_call(
        _peer_a_kernel,
        grid=(n // tm,),
        in_specs=[
            pl.BlockSpec((tm, D_MODEL), row),
            mod_spec,
            pl.BlockSpec((D_MODEL, PEER_HEADS * PEER_DQ), lambda i: (0, 0)),
            pl.BlockSpec((2 * PEER_HEADS, PEER_NKEYS, PEER_DQ // 2), lambda i: (0, 0, 0)),
        ],
        out_specs=[pl.BlockSpec((tm, D_MODEL), row), pl.BlockSpec((PEER_SLOTS, tm), col), pl.BlockSpec((PEER_SLOTS, tm), col)],
        out_shape=[
            jax.ShapeDtypeStruct((n, D_MODEL), BF16),
            jax.ShapeDtypeStruct((PEER_SLOTS, n), jnp.int32),
            jax.ShapeDtypeStruct((PEER_SLOTS, n), F32),
        ],
        scratch_shapes=[
            pltpu.VMEM((2 * PEER_HEADS * PEER_TOPK, tm), F32),
            pltpu.VMEM((2 * PEER_HEADS * PEER_TOPK, tm), jnp.int32),
            pltpu.VMEM((PEER_SLOTS, tm), F32),
        ],
        compiler_params=_cparams("parallel"),
        name="peer_retrieve",
    )(x, mod, wq, keys)


PEER_CI = 16


def _peer_b_kernel(h2_ref, ids_ref, gates_ref, ut_ref, v_ref, x_ref, mod_ref, g_ref, b_ref, o_ref,
                   gmat_scr, acc_scr, *, tm):
    c = pl.program_id(1)

    @pl.when(c == 0)
    def _():
        sub = _iota((PEER_NKEYS, PEER_SLOTS), 0)

        def tok8(n8, carry):
            base = pl.multiple_of(n8 * 8, 8)
            ids8 = ids_ref[pl.ds(base, 8), :]
            gates8 = gates_ref[pl.ds(base, 8), :]
            mats = []
            for r in range(8):
                idrow = ids8[r:r + 1, :]
                a = jnp.where(sub == (idrow >> 7), 1.0, 0.0).astype(BF16)
                b = jnp.where(sub == (idrow & (PEER_NKEYS - 1)), gates8[r:r + 1, :], 0.0).astype(BF16)
                mats.append(_dot(a, b, NT))
            gmat_scr[:, pl.ds(base, 8), :] = jnp.swapaxes(jnp.stack(mats, axis=0), 0, 1)
            return carry

        lax.fori_loop(0, tm // 8, tok8, 0)

    act = _dot(h2_ref[...], ut_ref[...])
    ws = []
    for j in range(PEER_CI):
        gj = gmat_scr[c * PEER_CI + j]
        ws.append((gj * jax.nn.gelu(act[:, j * LANES:(j + 1) * LANES])).astype(BF16))
    part = _dot(jnp.concatenate(ws, axis=1), v_ref[...])

    @pl.when(c == 0)
    def _():
        acc_scr[...] = part

    @pl.when(c > 0)
    def _():
        acc_scr[...] += part

    @pl.when(c == pl.num_programs(1) - 1)
    def _():
        r = DN_ALPHA * x_ref[...] + _mod_row(mod_ref, 5) * acc_scr[...]
        o_ref[...] = _ln(r) * g_ref[...] + b_ref[...]


def peer_mix(h2, ids, gates, ut, vv, x, mod, ln_g, ln_b, t):
    n = x.shape[0]
    tm = min(256, n)
    ec = PEER_CI * PEER_NKEYS
    row = lambda i, c: (i, 0)
    fixed = lambda i, c: (0, 0)
    mod, mod_spec = _mod_spec(mod, t, tm, n)
    return pl.pallas_call(
        functools.partial(_peer_b_kernel, tm=tm),
        grid=(n // tm, PEER_NKEYS // PEER_CI),
        in_specs=[
            pl.BlockSpec((tm, D_MODEL), row),
            pl.BlockSpec((tm, PEER_SLOTS), row),
            pl.BlockSpec((tm, PEER_SLOTS), row),
            pl.BlockSpec((D_MODEL, ec), lambda i, c: (0, c)),
            pl.BlockSpec((ec, D_MODEL), lambda i, c: (c, 0)),
            pl.BlockSpec((tm, D_MODEL), row),
            mod_spec,
            pl.BlockSpec((1, D_MODEL), fixed),
            pl.BlockSpec((1, D_MODEL), fixed),
        ],
        out_specs=pl.BlockSpec((tm, D_MODEL), row),
        out_shape=jax.ShapeDtypeStruct((n, D_MODEL), F32),
        scratch_shapes=[pltpu.VMEM((PEER_NKEYS, tm, PEER_NKEYS), F32), pltpu.VMEM((tm, D_MODEL), F32)],
        compiler_params=_cparams("parallel", "arbitrary"),
        name="peer_mix",
    )(h2, ids, gates, ut, vv, x, mod, ln_g.reshape(1, D_MODEL), ln_b.reshape(1, D_MODEL))


def _run_group(x3, mod_g, past_k, past_v, gdn_state, conv_state, wts):
    nb, t, _ = x3.shape
    n = nb * t
    cached = past_k is not None
    x = x3.reshape(n, D_MODEL)
    tp = -(-t // CHUNK) * CHUNK
    new_k, new_v, new_s, new_conv, new_sgu_v = [], [], [], [], []
    for l in range(DEPTH):
        mod = mod_g[l]
        pa, q, k, v, pc, pab = in_proj(x, mod, wts["w_main"][l], wts["w_ab"][l], t)
        ya, va = sgu(pa, wts["sgu_ln_g"][l], wts["sgu_ln_b"][l], wts["sgu_w"][l], wts["sgu_b"][l], t)
        lam_init = 0.8 - 0.6 * math.exp(-0.3 * l)
        lam = (jnp.exp(jnp.sum(wts["lam_q1"][l] * wts["lam_k1"][l]))
               - jnp.exp(jnp.sum(wts["lam_q2"][l] * wts["lam_k2"][l])) + lam_init)
        q3, k3, v3 = (a.reshape(nb, t, B_WIDTH) for a in (q, k, v))
        if cached:
            p_len = past_k.shape[2]
            tk = 256
            klen = p_len + t
            padk = -klen % tk
            kk = jnp.concatenate([past_k[l].reshape(nb, p_len, B_QK), k3, jnp.zeros((nb, padk, B_QK), F32)], axis=1)
            vv = jnp.concatenate([past_v[l].reshape(nb, p_len, B_WIDTH), v3, jnp.zeros((nb, padk, B_WIDTH), F32)], axis=1)
            yb = diff_attn(q3, kk, vv, lam, wts["diff_norm_g"][l], 1.0 - lam_init, q_off=p_len, klen=klen, tq=t, tk=tk)
        else:
            yb = diff_attn(q3, k3, v3, lam, wts["diff_norm_g"][l], 1.0 - lam_init, q_off=0, klen=t, tq=min(256, t), tk=256)
        pc3 = pc.reshape(nb, t, 1024)
        pab3 = pab.reshape(nb, t, LANES)
        if tp != t:
            pc3p = jnp.pad(pc3, ((0, 0), (0, tp - t), (0, 0)))
            pab3p = jnp.pad(pab3, ((0, 0), (0, tp - t), (0, 0)))
        else:
            pc3p, pab3p = pc3, pab3
        cst = conv_state[l] if cached else jnp.zeros((nb, CONV_W - 1, C_QKV), F32)
        s0 = gdn_state[l] if cached else jnp.zeros((nb, C_HEADS, C_DK, C_DV), F32)
        yc, s_fin = gdn(pc3p, pab3p, cst, s0, wts["conv_w"][l], wts["gdn_a_log"][l], wts["gdn_dt_bias"][l],
                        wts["gdn_norm_g"][l], t)
        yc = yc[:, :t].reshape(n, C_WIDTH)
        x = out_proj(x, ya, yb.reshape(n, B_WIDTH), yc, mod, wts["w_out"][l], wts["ln1_g"][l], wts["ln1_b"][l], t)
        h2, ids, gates = peer_retrieve(x, mod, wts["peer_wq"][l], wts["peer_keys"][l], t)
        x = peer_mix(h2, ids.T, gates.T, wts["expert_ut"][l], wts["expert_v"][l], x, mod,
                     wts["ln2_g"][l], wts["ln2_b"][l], t)
        new_k.append(k3.reshape(nb, t, B_HEADS, 2 * B_DK))
        new_v.append(v3.reshape(nb, t, B_HEADS, B_DV))
        new_s.append(s_fin)
        xp_tail = jnp.concatenate([cst, pc3[:, :, :C_QKV]], axis=1)[:, -(CONV_W - 1):] if t < CONV_W - 1 \
            else pc3[:, t - (CONV_W - 1):, :C_QKV]
        new_conv.append(xp_tail)
        new_sgu_v.append(va.reshape(nb, t, A_WIDTH))
    return (x.reshape(nb, t, D_MODEL), jnp.stack(new_k), jnp.stack(new_v), jnp.stack(new_s), jnp.stack(new_conv),
            jnp.stack(new_sgu_v))


def kernel(x_prompt, x_sample, cache_k, cache_v, state_gdn, state_conv, c_prompt, c_sample,
           w_ada, b_ada, w_in, sgu_ln_g, sgu_ln_b, sgu_w, sgu_b,
           lam_q1, lam_k1, lam_q2, lam_k2, diff_norm_g,
           conv_w, gdn_a_log, gdn_dt_bias, gdn_norm_g,
           w_out, ln1_g, ln1_b, peer_wq, peer_keys, expert_u, expert_v, ln2_g, ln2_b):
    nbp = x_prompt.shape[0]
    nbs = x_sample.shape[0]
    wts = dict(
        w_main=w_in[:, :, :P_MAIN].astype(BF16),
        w_ab=jnp.pad(w_in[:, :, P_MAIN:], ((0, 0), (0, 0), (0, LANES - (P_IN - P_MAIN)))).astype(BF16),
        sgu_ln_g=sgu_ln_g, sgu_ln_b=sgu_ln_b, sgu_w=sgu_w, sgu_b=sgu_b,
        lam_q1=lam_q1, lam_k1=lam_k1, lam_q2=lam_q2, lam_k2=lam_k2, diff_norm_g=diff_norm_g,
        conv_w=conv_w, gdn_a_log=gdn_a_log, gdn_dt_bias=gdn_dt_bias, gdn_norm_g=gdn_norm_g,
        w_out=w_out.astype(BF16), ln1_g=ln1_g, ln1_b=ln1_b,
        peer_wq=peer_wq.astype(BF16),
        peer_keys=peer_keys.reshape(DEPTH, 2 * PEER_HEADS, PEER_NKEYS, PEER_DQ // 2).astype(BF16),
        expert_ut=jnp.swapaxes(expert_u, 1, 2).astype(BF16),
        expert_v=expert_v.astype(BF16),
        ln2_g=ln2_g, ln2_b=ln2_b,
    )
    mod_all = ada_mod(jnp.concatenate([c_prompt, c_sample], axis=0), w_ada, b_ada)
    mod_all = mod_all.reshape(DEPTH, nbp + nbs, 6, D_MODEL)
    y_p, k_p, v_p, s_p, conv_p, _ = _run_group(x_prompt, mod_all[:, :nbp], None, None, None, None, wts)
    y_s, k_s, v_s, s_s, conv_s, sgu_s = _run_group(x_sample, mod_all[:, nbp:], cache_k, cache_v, state_gdn, state_conv, wts)
    return (y_p, y_s, k_p, v_p, s_p, conv_p, k_s, v_s, s_s, conv_s, sgu_s)
```

```python
import functools
import math

import jax
import jax.numpy as jnp
from jax import lax
from jax.experimental import pallas as pl
from jax.experimental.pallas import tpu as pltpu

F32 = jnp.float32
BF16 = jnp.bfloat16

D_MODEL = 1024
DEPTH = 4
CHUNK = 64
A_GROUPS = 4
A_GDIM = 64
A_WIDTH = 256
SGU_LEN = 128
B_HEADS = 4
B_DK = 64
B_DV = 128
B_WIDTH = 512
B_QK = 512
C_HEADS = 4
C_DK = 64
C_DV = 64
C_WIDTH = 256
CONV_W = 4
C_QKV = 768
P_MAIN = 3072
P_IN = 3080
PEER_HEADS = 8
PEER_NKEYS = 128
PEER_DQ = 256
PEER_TOPK = 16
PEER_SLOTS = PEER_HEADS * PEER_TOPK
DN_ALPHA = (2 * DEPTH) ** 0.25
LN_EPS = 1e-5
NEG = -1e30

LANES = 128
VMEM_LIMIT = 56 * 1024 * 1024

NN = (((1,), (0,)), ((), ()))
NT = (((1,), (1,)), ((), ()))


def _cparams(*sem):
    return pltpu.CompilerParams(dimension_semantics=sem, vmem_limit_bytes=VMEM_LIMIT)


def _dot(a, b, dims=NN):
    return lax.dot_general(a, b, dims, preferred_element_type=F32)


def _split(x, n):
    parts = []
    r = x
    for _ in range(n - 1):
        p = r.astype(BF16)
        parts.append(p)
        r = r - p.astype(F32)
    parts.append(r.astype(BF16))
    return parts


def _dot3(a, b, dims=NN):
    ah, al = _split(a, 2)
    bh, bl = _split(b, 2)
    return _dot(ah, bh, dims) + (_dot(ah, bl, dims) + _dot(al, bh, dims))


def _dot_exact_rhs(x, m_bf16, n=3, dims=NN):
    parts = _split(x, n)
    acc = _dot(parts[-1], m_bf16, dims)
    for p in parts[-2::-1]:
        acc = acc + _dot(p, m_bf16, dims)
    return acc


def _dot_exact_lhs(m_bf16, x, n=3, dims=NN):
    parts = _split(x, n)
    acc = _dot(m_bf16, parts[-1], dims)
    for p in parts[-2::-1]:
        acc = acc + _dot(m_bf16, p, dims)
    return acc


def _ln(x):
    mu = jnp.mean(x, axis=-1, keepdims=True)
    d = x - mu
    var = jnp.mean(d * d, axis=-1, keepdims=True)
    return d * lax.rsqrt(var + LN_EPS)


def _iota(shape, dim):
    return lax.broadcasted_iota(jnp.int32, shape, dim)


def _ada_kernel(c_ref, w_ref, b_ref, o_ref):
    c = c_ref[...]
    s = c * jax.nn.sigmoid(c)
    o_ref[0] = _dot3(s, w_ref[0]) + b_ref[0]


def ada_mod(c_all, w_ada, b_ada):
    nbt = c_all.shape[0]
    tn = 1536
    return pl.pallas_call(
        _ada_kernel,
        grid=(DEPTH, 6 * D_MODEL // tn),
        in_specs=[
            pl.BlockSpec((nbt, D_MODEL), lambda l, j: (0, 0)),
            pl.BlockSpec((1, D_MODEL, tn), lambda l, j: (l, 0, j)),
            pl.BlockSpec((1, 1, tn), lambda l, j: (l, 0, j)),
        ],
        out_specs=pl.BlockSpec((1, nbt, tn), lambda l, j: (l, 0, j)),
        out_shape=jax.ShapeDtypeStruct((DEPTH, nbt, 6 * D_MODEL), F32),
        compiler_params=_cparams("parallel", "parallel"),
        name="ada_mod",
    )(c_all, w_ada, b_ada.reshape(DEPTH, 1, 6 * D_MODEL))


def _in_kernel(x_ref, mod_ref, w_ref, wab_ref, pa_ref, q_ref, k_ref, v_ref, pc_ref, pab_ref):
    h = _ln(x_ref[...]) * (1.0 + mod_ref[0, 1:2, :]) + mod_ref[0, 0:1, :]
    hb = h.astype(BF16)
    pa_ref[...] = _dot(hb, w_ref[:, 0:512])
    q_ref[...] = _dot(hb, w_ref[:, 512:1024])
    k_ref[...] = _dot(hb, w_ref[:, 1024:1536])
    v_ref[...] = _dot(hb, w_ref[:, 1536:2048])
    pc_ref[...] = _dot(hb, w_ref[:, 2048:3072])
    pab_ref[...] = _dot(hb, wab_ref[...])


def in_proj(x, mod, w_main, w_ab, t):
    n = x.shape[0]
    tm = min(512, t)
    per = t // tm
    row = lambda i: (i, 0)
    fixed = lambda i: (0, 0)
    sds = lambda w: jax.ShapeDtypeStruct((n, w), F32)
    return pl.pallas_call(
        _in_kernel,
        grid=(n // tm,),
        in_specs=[
            pl.BlockSpec((tm, D_MODEL), row),
            pl.BlockSpec((1, 6, D_MODEL), lambda i: (i // per, 0, 0)),
            pl.BlockSpec((D_MODEL, P_MAIN), fixed),
            pl.BlockSpec((D_MODEL, LANES), fixed),
        ],
        out_specs=[pl.BlockSpec((tm, 512), row)] * 4 + [pl.BlockSpec((tm, 1024), row), pl.BlockSpec((tm, LANES), row)],
        out_shape=[sds(512)] * 4 + [sds(1024), sds(LANES)],
        compiler_params=_cparams("parallel"),
        name="in_proj",
    )(x, mod, w_main, w_ab)


def _group_ones(width, gdim):
    r = _iota((width, width), 0) // gdim
    c = _iota((width, width), 1) // gdim
    return jnp.where(r == c, 1.0, 0.0).astype(BF16)


def _sgu_kernel(pa_ref, g_ref, b_ref, w_ref, bs_ref, ya_ref, va_ref, *, blk, nblk):
    pa = pa_ref[...]
    u = jax.nn.gelu(pa[:, :A_WIDTH])
    v = jax.nn.gelu(pa[:, A_WIDTH:])
    ones = _group_ones(A_WIDTH, A_GDIM)
    mean = _dot_exact_rhs(v, ones) * (1.0 / A_GDIM)
    d = v - mean
    var = _dot_exact_rhs(d * d, ones) * (1.0 / A_GDIM)
    vn = d * lax.rsqrt(var + LN_EPS) * g_ref[...] + b_ref[...]
    va_ref[...] = vn
    ri = _iota((blk, blk), 0) // CHUNK
    ci = _iota((blk, blk), 1) // CHUNK
    lane_group = _iota((blk, A_WIDTH), 1) // A_GDIM
    wm = [jnp.where(ci <= ri, w_ref[g], 0.0).astype(BF16) for g in range(A_GROUPS)]
    for r in range(nblk):
        vb = vn[r * blk:(r + 1) * blk, :].astype(BF16)
        s = jnp.zeros((blk, A_WIDTH), F32)
        for g in range(A_GROUPS):
            s = jnp.where(lane_group == g, _dot(wm[g], vb), s)
        ya_ref[r * blk:(r + 1) * blk, :] = u[r * blk:(r + 1) * blk, :] * (s + bs_ref[...])


def sgu(pa, ln_g, ln_b, w_s, b_s, t):
    n = pa.shape[0]
    blk = min(SGU_LEN, t)
    assert t % blk == 0
    tm = min(512, n)
    w = w_s[:, :blk, :blk]
    bs = jnp.repeat(b_s.T[:blk], A_GDIM, axis=1)
    row = lambda i: (i, 0)
    return pl.pallas_call(
        functools.partial(_sgu_kernel, blk=blk, nblk=tm // blk),
        grid=(n // tm,),
        in_specs=[
            pl.BlockSpec((tm, 2 * A_WIDTH), row),
            pl.BlockSpec((1, A_WIDTH), lambda i: (0, 0)),
            pl.BlockSpec((1, A_WIDTH), lambda i: (0, 0)),
            pl.BlockSpec((A_GROUPS, blk, blk), lambda i: (0, 0, 0)),
            pl.BlockSpec((blk, A_WIDTH), lambda i: (0, 0)),
        ],
        out_specs=[pl.BlockSpec((tm, A_WIDTH), row)] * 2,
        out_shape=[jax.ShapeDtypeStruct((n, A_WIDTH), F32)] * 2,
        compiler_params=_cparams("parallel"),
        name="sgu",
    )(pa, ln_g.reshape(1, A_WIDTH), ln_b.reshape(1, A_WIDTH), w, bs)


ATTN_HEADS_PER_STEP = 2


def _attn_kernel(lam_ref, slope_ref, q_ref, k_ref, v_ref, g_ref, o_ref, *, tq, tk, q_off, klen, out_scale, nh):
    hg = pl.program_id(1)
    qi = pl.program_id(2)
    rows = 2 * tq
    log2e = 1.4426950408889634
    lane = _iota((tq, B_DV), 1)
    q_lo = q_off + qi * tq
    qpos = q_lo + (_iota((rows, tk), 0) & (tq - 1))
    col = _iota((rows, tk), 1)
    colf = _iota((1, tk), 1).astype(F32)
    limit = jnp.minimum(klen, ((q_lo + tq - 1) // CHUNK + 1) * CHUNK)
    nkb = (limit + tk - 1) // tk
    nfull = jnp.minimum(q_lo, klen) // tk
    heads = range(nh)
    hl = [slice(e * B_DV, (e + 1) * B_DV) for e in heads]
    slope2 = [slope_ref[hg * nh + e] * log2e for e in heads]
    qq = []
    for e in heads:
        q = q_ref[0, :, hl[e]] * (B_DK ** -0.5 * log2e)
        qq.append(jnp.concatenate([jnp.where(lane < B_DK, q, 0.0), jnp.where(lane >= B_DK, q, 0.0)],
                                  axis=0).astype(BF16))

    def scores(e, kb):
        start = pl.multiple_of(kb * tk, tk)
        return _dot(qq[e], k_ref[0, pl.ds(start, tk), hl[e]].astype(BF16), NT)

    def update(e, kb, s, m, l, acc):
        start = pl.multiple_of(kb * tk, tk)
        vblk = v_ref[0, pl.ds(start, tk), hl[e]].astype(BF16)
        m_new = jnp.maximum(m, jnp.max(s, axis=1, keepdims=True))
        alpha = jnp.exp2(m - m_new)
        p = jnp.exp2(s - m_new)
        l = alpha * l + jnp.sum(p, axis=1, keepdims=True)
        acc = alpha * acc + _dot(p.astype(BF16), vblk)
        return m_new, l, acc

    def masked(e, kb, s):
        kpos = kb * tk + col
        s = s + slope2[e] * (qpos - jnp.abs(qpos - kpos)).astype(F32)
        kchunk = jnp.where(kpos < klen, kpos // CHUNK, jnp.int32(1 << 30))
        return jnp.where(kchunk <= (qpos // CHUNK), s, NEG)

    def full_body(kb, carry):
        ss = [scores(e, kb) + slope2[e] * ((kb * tk).astype(F32) + colf) for e in heads]
        return tuple(update(e, kb, ss[e], *carry[e]) for e in heads)

    def tail_body(kb, carry):
        ss = [masked(e, kb, scores(e, kb)) for e in heads]
        return tuple(update(e, kb, ss[e], *carry[e]) for e in heads)

    init = tuple((jnp.full((rows, 1), NEG, F32), jnp.zeros((rows, 1), F32), jnp.zeros((rows, B_DV), F32))
                 for e in heads)
    carry = lax.fori_loop(0, nfull, full_body, init)
    carry = lax.fori_loop(nfull, nkb, tail_body, carry)
    for e in heads:
        m, l, acc = carry[e]
        o = acc / l
        ob = o[:tq] - lam_ref[0] * o[tq:]
        ms = jnp.mean(ob * ob, axis=-1, keepdims=True)
        o_ref[0, :, hl[e]] = ob * lax.rsqrt(ms + LN_EPS) * g_ref[...] * out_scale


def diff_attn(q, k, v, lam, norm_g, out_scale, *, q_off, klen, tq, tk):
    nb, t, _ = q.shape
    tkp = k.shape[1]
    assert klen >= q_off + t and tkp % tk == 0 and tkp >= klen and t % tq == 0
    slopes = jnp.exp2(-(8.0 / B_HEADS) * jnp.arange(1, B_HEADS + 1, dtype=F32))
    smem = pl.BlockSpec(memory_space=pltpu.SMEM)
    nh = ATTN_HEADS_PER_STEP
    return pl.pallas_call(
        functools.partial(_attn_kernel, tq=tq, tk=tk, q_off=q_off, klen=klen, out_scale=out_scale, nh=nh),
        grid=(nb, B_HEADS // nh, t // tq),
        in_specs=[
            smem, smem,
            pl.BlockSpec((1, tq, nh * B_DV), lambda b, h, i: (b, i, h)),
            pl.BlockSpec((1, tkp, nh * B_DV), lambda b, h, i: (b, 0, h)),
            pl.BlockSpec((1, tkp, nh * B_DV), lambda b, h, i: (b, 0, h)),
            pl.BlockSpec((1, B_DV), lambda b, h, i: (0, 0)),
        ],
        out_specs=pl.BlockSpec((1, tq, nh * B_DV), lambda b, h, i: (b, i, h)),
        out_shape=jax.ShapeDtypeStruct((nb, t, B_WIDTH), F32),
        compiler_params=_cparams("parallel", "parallel", "arbitrary"),
        name="diff_attn",
    )(lam.reshape(1), slopes, q, k, v, norm_g.reshape(1, B_DV))


GDN_ROWS = 4


def _gdn_kernel(pc_ref, pab_ref, cst_ref, cw_ref, alog_ref, dtb_ref, ng_ref, s0_ref,
                yc_ref, sfin_ref, prev_scr, *, t_valid, bb):
    c = pl.program_id(1)

    @pl.when(c == 0)
    def _():
        prev_scr[...] = cst_ref[...]
        sfin_ref[...] = s0_ref[...]

    rows = [_gdn_chunk(bi, c, pc_ref, pab_ref, cw_ref, alog_ref, dtb_ref, ng_ref, yc_ref, sfin_ref, prev_scr, t_valid)
            for bi in range(bb)]
    while rows:
        rows = [r for r in rows if next(r, True) is None]


def _gdn_chunk(bi, c, pc_ref, pab_ref, cw_ref, alog_ref, dtb_ref, ng_ref, yc_ref, sfin_ref, prev_scr, t_valid):
    W = C_WIDTH
    qkv = pc_ref[bi, :, 0:C_QKV]
    z = pc_ref[bi, :, C_QKV:C_QKV + W]
    ext = jnp.concatenate([prev_scr[bi], qkv], axis=0)
    prev_scr[bi] = qkv[CHUNK - 8:, :]
    conv = ext[5:5 + CHUNK] * cw_ref[0:1, :]
    for j in range(1, CONV_W):
        conv = conv + ext[5 + j:5 + j + CHUNK] * cw_ref[j:j + 1, :]
    conv = conv * jax.nn.sigmoid(conv)
    q, k, v = conv[:, 0:W], conv[:, W:2 * W], conv[:, 2 * W:3 * W]

    head_ones = _group_ones(W, C_DK)
    qss = _dot_exact_rhs(q * q, head_ones)
    kss = _dot_exact_rhs(k * k, head_ones)
    pab = pab_ref[bi]
    g_cols = -jnp.exp(alog_ref[...]) * jax.nn.softplus(pab + dtb_ref[...])
    b_cols = jax.nn.sigmoid(pab)
    er = _iota((LANES, W), 0)
    eh = _iota((LANES, W), 1) // C_DK
    g = _dot_exact_rhs(g_cols, jnp.where(er == eh, 1.0, 0.0).astype(BF16))
    beta = _dot_exact_rhs(b_cols, jnp.where(er == eh + C_HEADS, 1.0, 0.0).astype(BF16))
    yield
    q = q * lax.rsqrt(qss + 1e-6) * (C_DK ** -0.5)
    k = k * lax.rsqrt(kss + 1e-6)
    if t_valid % CHUNK:
        valid = (c * CHUNK + _iota((CHUNK, W), 0)) < t_valid
        q, k, v = (jnp.where(valid, a, 0.0) for a in (q, k, v))
        g = jnp.where(valid, g, 0.0)
        beta = jnp.where(valid, beta, 0.0)

    tri = jnp.where(_iota((CHUNK, CHUNK), 0) >= _iota((CHUNK, CHUNK), 1), 1.0, 0.0).astype(BF16)
    gc = _dot_exact_lhs(tri, g)
    yield
    eg = jnp.exp(gc)
    glast = gc[CHUNK - 1:CHUNK, :]
    kdec = k * jnp.exp(glast - gc)
    qg = q * eg
    kb = k * beta
    kbe = kb * eg
    vb = v * beta

    R = C_HEADS * CHUNK
    same = (_iota((R, W), 0) // CHUNK) == (_iota((R, W), 1) // C_DK)
    ii = _iota((CHUNK, W), 0)
    jj = _iota((CHUNK, W), 1) % CHUNK

    def bd(a):
        return jnp.where(same, jnp.concatenate([a] * C_HEADS, axis=0), 0.0)

    def split_bd(a):
        hi = a.astype(BF16)
        lo = (a - hi.astype(F32)).astype(BF16)
        return hi, lo, bd(hi.astype(F32)).astype(BF16), bd(lo.astype(F32)).astype(BF16)

    def mm3(a_hi, a_lo, b_hi, b_lo):
        return _dot(a_hi, b_hi) + (_dot(a_hi, b_lo) + _dot(a_lo, b_hi))

    kx = bd(k).astype(BF16)
    kk = _dot(kb.astype(BF16), kx, NT)
    qk = _dot(q.astype(BF16), kx, NT)
    pick = jnp.where(jj == 0, 1.0, 0.0).astype(BF16)
    g_col = _dot_exact_lhs(pick, bd(gc), dims=NT)
    yield
    lower = ii >= jj
    gamma = jnp.where(lower, jnp.exp(jnp.where(lower, gc - g_col, 0.0)), 0.0)
    nm = jnp.where(ii > jj, -(kk * gamma), 0.0)
    aqk = qk * gamma

    tinv = jnp.where(ii == jj, 1.0, 0.0) + nm
    p_hi, p_lo, pb_hi, pb_lo = split_bd(nm)
    for step in range(5):
        pw = mm3(p_hi, p_lo, pb_hi, pb_lo)
        yield
        p_hi, p_lo, pb_hi, pb_lo = split_bd(pw)
        t_hi, t_lo = _split(tinv, 2)
        tinv = tinv + mm3(t_hi, t_lo, pb_hi, pb_lo)
    _, _, vb_hi, vb_lo = split_bd(vb)
    _, _, ke_hi, ke_lo = split_bd(kbe)
    yield
    t_hi, t_lo = _split(tinv, 2)
    u = mm3(t_hi, t_lo, vb_hi, vb_lo)
    w = mm3(t_hi, t_lo, ke_hi, ke_lo)
    s = sfin_ref[bi]
    sb = s.astype(BF16)
    oq = _dot(qg.astype(BF16), sb)
    kdt = kdec.T.astype(BF16)
    yield
    v_new = u - _dot(w.astype(BF16), sb)
    yield
    vnb = v_new.astype(BF16)
    o = oq + _dot(aqk.astype(BF16), bd(v_new).astype(BF16))
    sfin_ref[bi] = s * jnp.exp(glast) + jnp.where(same, _dot(kdt, vnb), 0.0)
    yield
    ms = _dot_exact_rhs(o * o, head_ones) * (1.0 / C_DV)
    yield
    yc_ref[bi] = o * lax.rsqrt(ms + LN_EPS) * ng_ref[...] * (z * jax.nn.sigmoid(z))


def gdn(pc, pab, conv_state, s0, conv_w, a_log, dt_bias, norm_g, t_valid):
    nb, tp, _ = pc.shape
    cst = jnp.pad(conv_state, ((0, 0), (8 - (CONV_W - 1), 0), (0, 0)))
    cw = jnp.pad(conv_w, ((0, 8 - CONV_W), (0, 0)))
    pad_row = lambda a: jnp.pad(a.reshape(1, C_HEADS), ((0, 0), (0, LANES - C_HEADS)))
    eye = jnp.eye(C_HEADS, dtype=F32)
    s0_bd = jnp.einsum('bhde,hg->bhdge', s0, eye).reshape(nb, C_HEADS * C_DK, C_HEADS * C_DV)
    full = lambda b, c: (0, 0)
    bb = math.gcd(nb, GDN_ROWS)
    yc, sfin = pl.pallas_call(
        functools.partial(_gdn_kernel, t_valid=t_valid, bb=bb),
        grid=(nb // bb, tp // CHUNK),
        in_specs=[
            pl.BlockSpec((bb, CHUNK, 1024), lambda b, c: (b, c, 0)),
            pl.BlockSpec((bb, CHUNK, LANES), lambda b, c: (b, c, 0)),
            pl.BlockSpec((bb, 8, C_QKV), lambda b, c: (b, 0, 0)),
            pl.BlockSpec((8, C_QKV), full),
            pl.BlockSpec((1, LANES), full),
            pl.BlockSpec((1, LANES), full),
            pl.BlockSpec((1, C_WIDTH), full),
            pl.BlockSpec((bb, C_WIDTH, C_WIDTH), lambda b, c: (b, 0, 0)),
        ],
        out_specs=[
            pl.BlockSpec((bb, CHUNK, C_WIDTH), lambda b, c: (b, c, 0)),
            pl.BlockSpec((bb, C_WIDTH, C_WIDTH), lambda b, c: (b, 0, 0)),
        ],
        out_shape=[
            jax.ShapeDtypeStruct((nb, tp, C_WIDTH), F32),
            jax.ShapeDtypeStruct((nb, C_WIDTH, C_WIDTH), F32),
        ],
        scratch_shapes=[pltpu.VMEM((bb, 8, C_QKV), F32)],
        compiler_params=_cparams("parallel", "arbitrary"),
        name="gdn",
    )(pc, pab, cst, cw, pad_row(a_log), pad_row(dt_bias), jnp.tile(norm_g.reshape(1, C_DV), (1, C_HEADS)), s0_bd)
    s5 = sfin.reshape(nb, C_HEADS, C_DK, C_HEADS, C_DV)
    s_fin = jnp.stack([s5[:, h, :, h, :] for h in range(C_HEADS)], axis=1)
    return yc, s_fin


def _out_kernel(x_ref, ya_ref, yb_ref, yc_ref, mod_ref, w_ref, g_ref, b_ref, o_ref):
    y = _dot(ya_ref[...].astype(BF16), w_ref[0:256, :])
    y = y + _dot(yb_ref[...].astype(BF16), w_ref[256:768, :])
    y = y + _dot(yc_ref[...].astype(BF16), w_ref[768:1024, :])
    r = DN_ALPHA * x_ref[...] + mod_ref[0, 2:3, :] * y
    o_ref[...] = _ln(r) * g_ref[...] + b_ref[...]


def out_proj(x, ya, yb, yc, mod, w_out, ln_g, ln_b, t):
    n = x.shape[0]
    tm = min(512, t)
    per = t // tm
    row = lambda i: (i, 0)
    fixed = lambda i: (0, 0)
    return pl.pallas_call(
        _out_kernel,
        grid=(n // tm,),
        in_specs=[
            pl.BlockSpec((tm, D_MODEL), row),
            pl.BlockSpec((tm, A_WIDTH), row),
            pl.BlockSpec((tm, B_WIDTH), row),
            pl.BlockSpec((tm, C_WIDTH), row),
            pl.BlockSpec((1, 6, D_MODEL), lambda i: (i // per, 0, 0)),
            pl.BlockSpec((D_MODEL, D_MODEL), fixed),
            pl.BlockSpec((1, D_MODEL), fixed),
            pl.BlockSpec((1, D_MODEL), fixed),
        ],
        out_specs=pl.BlockSpec((tm, D_MODEL), row),
        out_shape=jax.ShapeDtypeStruct((n, D_MODEL), F32),
        compiler_params=_cparams("parallel"),
        name="out_proj",
    )(x, ya, yb, yc, mod, w_out, ln_g.reshape(1, D_MODEL), ln_b.reshape(1, D_MODEL))


def _extract_top(vals, order, payload, out_s_ref, out_p_ref, base, lanes):
    big = 1e9
    for j in range(PEER_TOPK):
        m = jnp.max(vals, axis=0, keepdims=True)
        tied = jnp.where(vals == m, order, big)
        pos = jnp.min(tied, axis=0, keepdims=True)
        out_s_ref[base + j:base + j + 1, lanes] = m
        if payload is None:
            out_p_ref[base + j:base + j + 1, lanes] = pos
        else:
            out_p_ref[base + j:base + j + 1, lanes] = jnp.max(jnp.where(tied == pos, payload, -1.0), axis=0, keepdims=True)
        vals = jnp.where(tied == pos, -jnp.inf, vals)


def _mod_row(mod_ref, i):
    return mod_ref[0, i:i + 1, :] if len(mod_ref.shape) == 3 else mod_ref[0, i]


def _mod_spec(mod, t, tm, n):
    if tm > t:
        mod = jnp.repeat(mod, t, axis=0).reshape(n // tm, tm, 6, D_MODEL).transpose(0, 2, 1, 3)
        return mod, pl.BlockSpec((1, 6, tm, D_MODEL), lambda i, *_: (i, 0, 0, 0))
    per = t // tm
    return mod, pl.BlockSpec((1, 6, D_MODEL), lambda i, *_: (i // per, 0, 0))


def _peer_a_kernel(x_ref, mod_ref, wq_ref, keys_ref, h2_ref, ids_ref, gates_ref, ts_scr, ti_scr, bs_scr, id_scr):
    tm = x_ref.shape[0]
    h2 = _ln(x_ref[...]) * (1.0 + _mod_row(mod_ref, 4)) + _mod_row(mod_ref, 3)
    hb = h2.astype(BF16)
    h2_ref[...] = hb
    groups = [slice(g * LANES, (g + 1) * LANES) for g in range(tm // LANES)]
    krow = _iota((PEER_NKEYS, LANES), 0).astype(F32)
    for hp in range(2 * PEER_HEADS):
        qhp = _dot(hb, wq_ref[:, hp * LANES:(hp + 1) * LANES]).astype(BF16)
        st = _dot(keys_ref[hp], qhp, NT)
        for lanes in groups:
            _extract_top(st[:, lanes], krow, None, ts_scr, ti_scr, hp * PEER_TOPK, lanes)
    r8i = _iota((8, LANES), 0)
    r8 = r8i.astype(F32)
    for h in range(PEER_HEADS):
        b0 = 2 * h * PEER_TOPK
        b1 = b0 + PEER_TOPK
        for lanes in groups:
            cand, ids, order = [], [], []

            def add(sa, sb, rows_a, lo, hi):
                va = ts_scr[b0 + sa[0]:b0 + sa[0] + sa[1], lanes]
                vb = ts_scr[b1 + sb[0]:b1 + sb[0] + sb[1], lanes]
                ia = ti_scr[b0 + sa[0]:b0 + sa[0] + sa[1], lanes]
                ib = ti_scr[b1 + sb[0]:b1 + sb[0] + sb[1], lanes]
                s = va + vb
                if (lo, hi) != (0, 8):
                    s = jnp.where(r8i >= lo, jnp.where(r8i < hi, s, -jnp.inf), -jnp.inf)
                cand.append(s)
                ids.append(ia * float(PEER_NKEYS) + ib)
                order.append((sa[0] + r8) * float(PEER_TOPK) + sb[0] if rows_a else sa[0] * PEER_TOPK + sb[0] + r8)

            add((0, 1), (0, 8), False, 0, 8)
            add((0, 1), (8, 8), False, 0, 8)
            add((1, 1), (0, 8), False, 0, 8)
            add((8, 8), (0, 1), True, 0, 8)
            for b in range(5):
                add((0, 8), (b, 1), True, 2, PEER_TOPK // (b + 1))
            _extract_top(jnp.concatenate(cand, axis=0), jnp.concatenate(order, axis=0), jnp.concatenate(ids, axis=0),
                         bs_scr, id_scr, h * PEER_TOPK, lanes)
        bs = bs_scr[h * PEER_TOPK:(h + 1) * PEER_TOPK, :]
        e = jnp.exp(bs - bs[0:1, :])
        gates_ref[h * PEER_TOPK:(h + 1) * PEER_TOPK, :] = e / jnp.sum(e, axis=0, keepdims=True)
    ids_ref[...] = id_scr[...].astype(jnp.int32)


def peer_retrieve(x, mod, wq, keys, t):
    n = x.shape[0]
    tm = min(256, n)
    row = lambda i: (i, 0)
    col = lambda i: (0, i)
    mod, mod_spec = _mod_spec(mod, t, tm, n)
    return pl.pallas_call(
        _peer_a_kernel,
        grid=(n // tm,),
        in_specs=[
            pl.BlockSpec((tm, D_MODEL), row),
            mod_spec,
            pl.BlockSpec((D_MODEL, PEER_HEADS * PEER_DQ), lambda i: (0, 0)),
            pl.BlockSpec((2 * PEER_HEADS, PEER_NKEYS, PEER_DQ // 2), lambda i: (0, 0, 0)),
        ],
        out_specs=[pl.BlockSpec((tm, D_MODEL), row), pl.BlockSpec((PEER_SLOTS, tm), col), pl.BlockSpec((PEER_SLOTS, tm), col)],
        out_shape=[
            jax.ShapeDtypeStruct((n, D_MODEL), BF16),
            jax.ShapeDtypeStruct((PEER_SLOTS, n), jnp.int32),
            jax.ShapeDtypeStruct((PEER_SLOTS, n), F32),
        ],
        scratch_shapes=[
            pltpu.VMEM((2 * PEER_HEADS * PEER_TOPK, tm), F32),
            pltpu.VMEM((2 * PEER_HEADS * PEER_TOPK, tm), F32),
            pltpu.VMEM((PEER_SLOTS, tm), F32),
            pltpu.VMEM((PEER_SLOTS, tm), F32),
        ],
        compiler_params=_cparams("parallel"),
        name="peer_retrieve",
    )(x, mod, wq, keys)


PEER_CI = 16


GATE_TOK = 16


def _build_gates(ids_ref, gates_ref, base, gmat_scr, slot):
    sub = _iota((PEER_NKEYS, PEER_SLOTS), 0).astype(F32).astype(BF16)
    ids = ids_ref[pl.ds(base, GATE_TOK), :]
    gates = gates_ref[pl.ds(base, GATE_TOK), :].astype(BF16)
    key1 = (ids >> 7).astype(F32).astype(BF16)
    key2 = (ids & (PEER_NKEYS - 1)).astype(F32).astype(BF16)
    one = jnp.ones((PEER_NKEYS, PEER_SLOTS), BF16)
    zero = jnp.zeros((PEER_NKEYS, PEER_SLOTS), BF16)
    mats = []
    for r in range(GATE_TOK):
        a = jnp.where(sub == key1[r:r + 1, :], one, zero)
        b = jnp.where(sub == key2[r:r + 1, :], jnp.broadcast_to(gates[r:r + 1, :], zero.shape), zero)
        mats.append(_dot(a, b, NT))
    g = jnp.swapaxes(jnp.stack(mats, axis=0), 0, 1)
    gmat_scr[slot, :, pl.ds(base, GATE_TOK), :] = g.astype(BF16)


def _peer_b_kernel(h2_ref, ids_ref, gates_ref, ids_next_ref, gates_next_ref, ut_ref, v_ref, x_ref, mod_ref,
                   g_ref, b_ref, o_ref, gmat_scr, acc_scr, *, tm):
    i = pl.program_id(0)
    c = pl.program_id(1)
    nch = PEER_NKEYS // PEER_CI
    slot = i % 2

    @pl.when(jnp.logical_and(i == 0, c == 0))
    def _():
        def group(n, carry):
            _build_gates(ids_ref, gates_ref, pl.multiple_of(n * GATE_TOK, GATE_TOK), gmat_scr, 0)
            return carry

        lax.fori_loop(0, tm // GATE_TOK, group, 0)

    share = tm // nch
    for r in range(share // GATE_TOK):
        _build_gates(ids_next_ref, gates_next_ref, pl.multiple_of(c * share + r * GATE_TOK, GATE_TOK), gmat_scr, 1 - slot)

    act = _dot(h2_ref[...], ut_ref[...])
    ws = []
    for j in range(PEER_CI):
        gj = gmat_scr[slot, c * PEER_CI + j].astype(F32)
        ws.append((gj * jax.nn.gelu(act[:, j * LANES:(j + 1) * LANES])).astype(BF16))
    part = _dot(jnp.concatenate(ws, axis=1), v_ref[...])

    @pl.when(c == 0)
    def _():
        acc_scr[...] = part

    @pl.when(c > 0)
    def _():
        acc_scr[...] += part

    @pl.when(c == pl.num_programs(1) - 1)
    def _():
        r = DN_ALPHA * x_ref[...] + _mod_row(mod_ref, 5) * acc_scr[...]
        o_ref[...] = _ln(r) * g_ref[...] + b_ref[...]


def peer_mix(h2, ids, gates, ut, vv, x, mod, ln_g, ln_b, t):
    n = x.shape[0]
    tm = min(256, n)
    ec = PEER_CI * PEER_NKEYS
    row = lambda i, c: (i, 0)
    fixed = lambda i, c: (0, 0)
    last = n // tm - 1
    nxt = lambda i, c: (jnp.minimum(i + 1, last), 0)
    mod, mod_spec = _mod_spec(mod, t, tm, n)
    assert tm % (GATE_TOK * (PEER_NKEYS // PEER_CI)) == 0
    return pl.pallas_call(
        functools.partial(_peer_b_kernel, tm=tm),
        grid=(n // tm, PEER_NKEYS // PEER_CI),
        in_specs=[
            pl.BlockSpec((tm, D_MODEL), row),
            pl.BlockSpec((tm, PEER_SLOTS), row),
            pl.BlockSpec((tm, PEER_SLOTS), row),
            pl.BlockSpec((tm, PEER_SLOTS), nxt),
            pl.BlockSpec((tm, PEER_SLOTS), nxt),
            pl.BlockSpec((D_MODEL, ec), lambda i, c: (0, c)),
            pl.BlockSpec((ec, D_MODEL), lambda i, c: (c, 0)),
            pl.BlockSpec((tm, D_MODEL), row),
            mod_spec,
            pl.BlockSpec((1, D_MODEL), fixed),
            pl.BlockSpec((1, D_MODEL), fixed),
        ],
        out_specs=pl.BlockSpec((tm, D_MODEL), row),
        out_shape=jax.ShapeDtypeStruct((n, D_MODEL), F32),
        scratch_shapes=[pltpu.VMEM((2, PEER_NKEYS, tm, PEER_NKEYS), BF16), pltpu.VMEM((tm, D_MODEL), F32)],
        compiler_params=_cparams("arbitrary", "arbitrary"),
        name="peer_mix",
    )(h2, ids, gates, ids, gates, ut, vv, x, mod, ln_g.reshape(1, D_MODEL), ln_b.reshape(1, D_MODEL))


def _run_group(x3, mod_g, past_k, past_v, gdn_state, conv_state, wts):
    nb, t, _ = x3.shape
    n = nb * t
    cached = past_k is not None
    x = x3.reshape(n, D_MODEL)
    tp = -(-t // CHUNK) * CHUNK
    new_k, new_v, new_s, new_conv, new_sgu_v = [], [], [], [], []
    for l in range(DEPTH):
        mod = mod_g[l]
        pa, q, k, v, pc, pab = in_proj(x, mod, wts["w_main"][l], wts["w_ab"][l], t)
        ya, va = sgu(pa, wts["sgu_ln_g"][l], wts["sgu_ln_b"][l], wts["sgu_w"][l], wts["sgu_b"][l], t)
        lam_init = 0.8 - 0.6 * math.exp(-0.3 * l)
        lam = (jnp.exp(jnp.sum(wts["lam_q1"][l] * wts["lam_k1"][l]))
               - jnp.exp(jnp.sum(wts["lam_q2"][l] * wts["lam_k2"][l])) + lam_init)
        q3, k3, v3 = (a.reshape(nb, t, B_WIDTH) for a in (q, k, v))
        if cached:
            p_len = past_k.shape[2]
            tk = 256
            klen = p_len + t
            padk = -klen % tk
            kk = jnp.concatenate([past_k[l].reshape(nb, p_len, B_QK), k3, jnp.zeros((nb, padk, B_QK), F32)], axis=1)
            vv = jnp.concatenate([past_v[l].reshape(nb, p_len, B_WIDTH), v3, jnp.zeros((nb, padk, B_WIDTH), F32)], axis=1)
            yb = diff_attn(q3, kk, vv, lam, wts["diff_norm_g"][l], 1.0 - lam_init, q_off=p_len, klen=klen, tq=t, tk=tk)
        else:
            yb = diff_attn(q3, k3, v3, lam, wts["diff_norm_g"][l], 1.0 - lam_init, q_off=0, klen=t, tq=min(256, t), tk=256)
        pc3 = pc.reshape(nb, t, 1024)
        pab3 = pab.reshape(nb, t, LANES)
        if tp != t:
            pc3p = jnp.pad(pc3, ((0, 0), (0, tp - t), (0, 0)))
            pab3p = jnp.pad(pab3, ((0, 0), (0, tp - t), (0, 0)))
        else:
            pc3p, pab3p = pc3, pab3
        cst = conv_state[l] if cached else jnp.zeros((nb, CONV_W - 1, C_QKV), F32)
        s0 = gdn_state[l] if cached else jnp.zeros((nb, C_HEADS, C_DK, C_DV), F32)
        yc, s_fin = gdn(pc3p, pab3p, cst, s0, wts["conv_w"][l], wts["gdn_a_log"][l], wts["gdn_dt_bias"][l],
                        wts["gdn_norm_g"][l], t)
        yc = yc[:, :t].reshape(n, C_WIDTH)
        x = out_proj(x, ya, yb.reshape(n, B_WIDTH), yc, mod, wts["w_out"][l], wts["ln1_g"][l], wts["ln1_b"][l], t)
        h2, ids, gates = peer_retrieve(x, mod, wts["peer_wq"][l], wts["peer_keys"][l], t)
        x = peer_mix(h2, ids.T, gates.T, wts["expert_ut"][l], wts["expert_v"][l], x, mod,
                     wts["ln2_g"][l], wts["ln2_b"][l], t)
        new_k.append(k3.reshape(nb, t, B_HEADS, 2 * B_DK))
        new_v.append(v3.reshape(nb, t, B_HEADS, B_DV))
        new_s.append(s_fin)
        xp_tail = jnp.concatenate([cst, pc3[:, :, :C_QKV]], axis=1)[:, -(CONV_W - 1):] if t < CONV_W - 1 \
            else pc3[:, t - (CONV_W - 1):, :C_QKV]
        new_conv.append(xp_tail)
        new_sgu_v.append(va.reshape(nb, t, A_WIDTH))
    return (x.reshape(nb, t, D_MODEL), jnp.stack(new_k), jnp.stack(new_v), jnp.stack(new_s), jnp.stack(new_conv),
            jnp.stack(new_sgu_v))


def kernel(x_prompt, x_sample, cache_k, cache_v, state_gdn, state_conv, c_prompt, c_sample,
           w_ada, b_ada, w_in, sgu_ln_g, sgu_ln_b, sgu_w, sgu_b,
           lam_q1, lam_k1, lam_q2, lam_k2, diff_norm_g,
           conv_w, gdn_a_log, gdn_dt_bias, gdn_norm_g,
           w_out, ln1_g, ln1_b, peer_wq, peer_keys, expert_u, expert_v, ln2_g, ln2_b):
    nbp = x_prompt.shape[0]
    nbs = x_sample.shape[0]
    wts = dict(
        w_main=w_in[:, :, :P_MAIN].astype(BF16),
        w_ab=jnp.pad(w_in[:, :, P_MAIN:], ((0, 0), (0, 0), (0, LANES - (P_IN - P_MAIN)))).astype(BF16),
        sgu_ln_g=sgu_ln_g, sgu_ln_b=sgu_ln_b, sgu_w=sgu_w, sgu_b=sgu_b,
        lam_q1=lam_q1, lam_k1=lam_k1, lam_q2=lam_q2, lam_k2=lam_k2, diff_norm_g=diff_norm_g,
        conv_w=conv_w, gdn_a_log=gdn_a_log, gdn_dt_bias=gdn_dt_bias, gdn_norm_g=gdn_norm_g,
        w_out=w_out.astype(BF16), ln1_g=ln1_g, ln1_b=ln1_b,
        peer_wq=peer_wq.astype(BF16),
        peer_keys=peer_keys.reshape(DEPTH, 2 * PEER_HEADS, PEER_NKEYS, PEER_DQ // 2).astype(BF16),
        expert_ut=jnp.swapaxes(expert_u, 1, 2).astype(BF16),
        expert_v=expert_v.astype(BF16),
        ln2_g=ln2_g, ln2_b=ln2_b,
    )
    mod_all = ada_mod(jnp.concatenate([c_prompt, c_sample], axis=0), w_ada, b_ada)
    mod_all = mod_all.reshape(DEPTH, nbp + nbs, 6, D_MODEL)
    y_p, k_p, v_p, s_p, conv_p, _ = _run_group(x_prompt, mod_all[:, :nbp], None, None, None, None, wts)
    y_s, k_s, v_s, s_s, conv_s, sgu_s = _run_group(x_sample, mod_all[:, nbp:], cache_k, cache_v, state_gdn, state_conv, wts)
    return (y_p, y_s, k_p, v_p, s_p, conv_p, k_s, v_s, s_s, conv_s, sgu_s)
```

```python
import functools
import math

import jax
import jax.numpy as jnp
from jax import lax
from jax.experimental import pallas as pl
from jax.experimental.pallas import tpu as pltpu

F32 = jnp.float32
BF16 = jnp.bfloat16

D_MODEL = 1024
DEPTH = 4
CHUNK = 64
A_GROUPS = 4
A_GDIM = 64
A_WIDTH = 256
SGU_LEN = 128
B_HEADS = 4
B_DK = 64
B_DV = 128
B_WIDTH = 512
B_QK = 512
C_HEADS = 4
C_DK = 64
C_DV = 64
C_WIDTH = 256
CONV_W = 4
C_QKV = 768
P_MAIN = 3072
P_IN = 3080
PEER_HEADS = 8
PEER_NKEYS = 128
PEER_DQ = 256
PEER_TOPK = 16
PEER_SLOTS = PEER_HEADS * PEER_TOPK
DN_ALPHA = (2 * DEPTH) ** 0.25
LN_EPS = 1e-5
NEG = -1e30

LANES = 128
VMEM_LIMIT = 56 * 1024 * 1024

NN = (((1,), (0,)), ((), ()))
NT = (((1,), (1,)), ((), ()))


def _cparams(*sem):
    return pltpu.CompilerParams(dimension_semantics=sem, vmem_limit_bytes=VMEM_LIMIT)


def _dot(a, b, dims=NN):
    return lax.dot_general(a, b, dims, preferred_element_type=F32)


def _split(x, n):
    parts = []
    r = x
    for _ in range(n - 1):
        p = r.astype(BF16)
        parts.append(p)
        r = r - p.astype(F32)
    parts.append(r.astype(BF16))
    return parts


def _dot3(a, b, dims=NN):
    ah, al = _split(a, 2)
    bh, bl = _split(b, 2)
    return _dot(ah, bh, dims) + (_dot(ah, bl, dims) + _dot(al, bh, dims))


def _dot_exact_rhs(x, m_bf16, n=3, dims=NN):
    parts = _split(x, n)
    acc = _dot(parts[-1], m_bf16, dims)
    for p in parts[-2::-1]:
        acc = acc + _dot(p, m_bf16, dims)
    return acc


def _dot_exact_lhs(m_bf16, x, n=3, dims=NN):
    parts = _split(x, n)
    acc = _dot(m_bf16, parts[-1], dims)
    for p in parts[-2::-1]:
        acc = acc + _dot(m_bf16, p, dims)
    return acc


def _ln(x):
    mu = jnp.mean(x, axis=-1, keepdims=True)
    d = x - mu
    var = jnp.mean(d * d, axis=-1, keepdims=True)
    return d * lax.rsqrt(var + LN_EPS)


def _iota(shape, dim):
    return lax.broadcasted_iota(jnp.int32, shape, dim)


def _ada_kernel(c_ref, w_ref, b_ref, o_ref):
    c = c_ref[...]
    s = c * jax.nn.sigmoid(c)
    o_ref[0] = _dot3(s, w_ref[0]) + b_ref[0]


def ada_mod(c_all, w_ada, b_ada):
    nbt = c_all.shape[0]
    tn = 1536
    return pl.pallas_call(
        _ada_kernel,
        grid=(DEPTH, 6 * D_MODEL // tn),
        in_specs=[
            pl.BlockSpec((nbt, D_MODEL), lambda l, j: (0, 0)),
            pl.BlockSpec((1, D_MODEL, tn), lambda l, j: (l, 0, j)),
            pl.BlockSpec((1, 1, tn), lambda l, j: (l, 0, j)),
        ],
        out_specs=pl.BlockSpec((1, nbt, tn), lambda l, j: (l, 0, j)),
        out_shape=jax.ShapeDtypeStruct((DEPTH, nbt, 6 * D_MODEL), F32),
        compiler_params=_cparams("parallel", "parallel"),
        name="ada_mod",
    )(c_all, w_ada, b_ada.reshape(DEPTH, 1, 6 * D_MODEL))


def _in_kernel(x_ref, mod_ref, w_ref, wab_ref, pa_ref, q_ref, k_ref, v_ref, pc_ref, pab_ref):
    h = _ln(x_ref[...]) * (1.0 + mod_ref[0, 1:2, :]) + mod_ref[0, 0:1, :]
    hb = h.astype(BF16)
    pa_ref[...] = _dot(hb, w_ref[:, 0:512])
    q_ref[...] = _dot(hb, w_ref[:, 512:1024])
    k_ref[...] = _dot(hb, w_ref[:, 1024:1536])
    v_ref[...] = _dot(hb, w_ref[:, 1536:2048])
    pc_ref[...] = _dot(hb, w_ref[:, 2048:3072])
    pab_ref[...] = _dot(hb, wab_ref[...])


def in_proj(x, mod, w_main, w_ab, t):
    n = x.shape[0]
    tm = min(512, t)
    per = t // tm
    row = lambda i: (i, 0)
    fixed = lambda i: (0, 0)
    sds = lambda w: jax.ShapeDtypeStruct((n, w), F32)
    return pl.pallas_call(
        _in_kernel,
        grid=(n // tm,),
        in_specs=[
            pl.BlockSpec((tm, D_MODEL), row),
            pl.BlockSpec((1, 6, D_MODEL), lambda i: (i // per, 0, 0)),
            pl.BlockSpec((D_MODEL, P_MAIN), fixed),
            pl.BlockSpec((D_MODEL, LANES), fixed),
        ],
        out_specs=[pl.BlockSpec((tm, 512), row)] * 4 + [pl.BlockSpec((tm, 1024), row), pl.BlockSpec((tm, LANES), row)],
        out_shape=[sds(512)] * 4 + [sds(1024), sds(LANES)],
        compiler_params=_cparams("parallel"),
        name="in_proj",
    )(x, mod, w_main, w_ab)


def _group_ones(width, gdim):
    r = _iota((width, width), 0) // gdim
    c = _iota((width, width), 1) // gdim
    return jnp.where(r == c, 1.0, 0.0).astype(BF16)


def _sgu_kernel(pa_ref, g_ref, b_ref, w_ref, bs_ref, ya_ref, va_ref, *, blk, nblk):
    pa = pa_ref[...]
    u = jax.nn.gelu(pa[:, :A_WIDTH])
    v = jax.nn.gelu(pa[:, A_WIDTH:])
    ones = _group_ones(A_WIDTH, A_GDIM)
    mean = _dot_exact_rhs(v, ones) * (1.0 / A_GDIM)
    d = v - mean
    var = _dot_exact_rhs(d * d, ones) * (1.0 / A_GDIM)
    vn = d * lax.rsqrt(var + LN_EPS) * g_ref[...] + b_ref[...]
    va_ref[...] = vn
    ri = _iota((blk, blk), 0) // CHUNK
    ci = _iota((blk, blk), 1) // CHUNK
    lane_group = _iota((blk, A_WIDTH), 1) // A_GDIM
    wm = [jnp.where(ci <= ri, w_ref[g], 0.0).astype(BF16) for g in range(A_GROUPS)]
    for r in range(nblk):
        vb = vn[r * blk:(r + 1) * blk, :].astype(BF16)
        s = jnp.zeros((blk, A_WIDTH), F32)
        for g in range(A_GROUPS):
            s = jnp.where(lane_group == g, _dot(wm[g], vb), s)
        ya_ref[r * blk:(r + 1) * blk, :] = u[r * blk:(r + 1) * blk, :] * (s + bs_ref[...])


def sgu(pa, ln_g, ln_b, w_s, b_s, t):
    n = pa.shape[0]
    blk = min(SGU_LEN, t)
    assert t % blk == 0
    tm = min(512, n)
    w = w_s[:, :blk, :blk]
    bs = jnp.repeat(b_s.T[:blk], A_GDIM, axis=1)
    row = lambda i: (i, 0)
    return pl.pallas_call(
        functools.partial(_sgu_kernel, blk=blk, nblk=tm // blk),
        grid=(n // tm,),
        in_specs=[
            pl.BlockSpec((tm, 2 * A_WIDTH), row),
            pl.BlockSpec((1, A_WIDTH), lambda i: (0, 0)),
            pl.BlockSpec((1, A_WIDTH), lambda i: (0, 0)),
            pl.BlockSpec((A_GROUPS, blk, blk), lambda i: (0, 0, 0)),
            pl.BlockSpec((blk, A_WIDTH), lambda i: (0, 0)),
        ],
        out_specs=[pl.BlockSpec((tm, A_WIDTH), row)] * 2,
        out_shape=[jax.ShapeDtypeStruct((n, A_WIDTH), F32)] * 2,
        compiler_params=_cparams("parallel"),
        name="sgu",
    )(pa, ln_g.reshape(1, A_WIDTH), ln_b.reshape(1, A_WIDTH), w, bs)


ATTN_HEADS_PER_STEP = 2


def _attn_kernel(lam_ref, slope_ref, q_ref, k_ref, v_ref, g_ref, o_ref, *, tq, tk, q_off, klen, out_scale, nh):
    hg = pl.program_id(1)
    qi = pl.program_id(2)
    rows = 2 * tq
    log2e = 1.4426950408889634
    lane = _iota((tq, B_DV), 1)
    q_lo = q_off + qi * tq
    qpos = q_lo + (_iota((rows, tk), 0) & (tq - 1))
    col = _iota((rows, tk), 1)
    colf = _iota((1, tk), 1).astype(F32)
    limit = jnp.minimum(klen, ((q_lo + tq - 1) // CHUNK + 1) * CHUNK)
    nkb = (limit + tk - 1) // tk
    nfull = jnp.minimum(q_lo, klen) // tk
    heads = range(nh)
    hl = [slice(e * B_DV, (e + 1) * B_DV) for e in heads]
    slope2 = [slope_ref[hg * nh + e] * log2e for e in heads]
    qq = []
    for e in heads:
        q = q_ref[0, :, hl[e]] * (B_DK ** -0.5 * log2e)
        qq.append(jnp.concatenate([jnp.where(lane < B_DK, q, 0.0), jnp.where(lane >= B_DK, q, 0.0)],
                                  axis=0).astype(BF16))

    def scores(e, kb):
        start = pl.multiple_of(kb * tk, tk)
        return _dot(qq[e], k_ref[0, pl.ds(start, tk), hl[e]].astype(BF16), NT)

    def update(e, kb, s, m, l, acc):
        start = pl.multiple_of(kb * tk, tk)
        vblk = v_ref[0, pl.ds(start, tk), hl[e]].astype(BF16)
        m_new = jnp.maximum(m, jnp.max(s, axis=1, keepdims=True))
        alpha = jnp.exp2(m - m_new)
        p = jnp.exp2(s - m_new)
        l = alpha * l + jnp.sum(p, axis=1, keepdims=True)
        acc = alpha * acc + _dot(p.astype(BF16), vblk)
        return m_new, l, acc

    def masked(e, kb, s):
        kpos = kb * tk + col
        s = s + slope2[e] * (qpos - jnp.abs(qpos - kpos)).astype(F32)
        kchunk = jnp.where(kpos < klen, kpos // CHUNK, jnp.int32(1 << 30))
        return jnp.where(kchunk <= (qpos // CHUNK), s, NEG)

    def full_body(kb, carry):
        ss = [scores(e, kb) + slope2[e] * ((kb * tk).astype(F32) + colf) for e in heads]
        return tuple(update(e, kb, ss[e], *carry[e]) for e in heads)

    def tail_body(kb, carry):
        ss = [masked(e, kb, scores(e, kb)) for e in heads]
        return tuple(update(e, kb, ss[e], *carry[e]) for e in heads)

    init = tuple((jnp.full((rows, 1), NEG, F32), jnp.zeros((rows, 1), F32), jnp.zeros((rows, B_DV), F32))
                 for e in heads)
    carry = lax.fori_loop(0, nfull, full_body, init)
    carry = lax.fori_loop(nfull, nkb, tail_body, carry)
    for e in heads:
        m, l, acc = carry[e]
        o = acc / l
        ob = o[:tq] - lam_ref[0] * o[tq:]
        ms = jnp.mean(ob * ob, axis=-1, keepdims=True)
        o_ref[0, :, hl[e]] = ob * lax.rsqrt(ms + LN_EPS) * g_ref[...] * out_scale


def diff_attn(q, k, v, lam, norm_g, out_scale, *, q_off, klen, tq, tk):
    nb, t, _ = q.shape
    tkp = k.shape[1]
    assert klen >= q_off + t and tkp % tk == 0 and tkp >= klen and t % tq == 0
    slopes = jnp.exp2(-(8.0 / B_HEADS) * jnp.arange(1, B_HEADS + 1, dtype=F32))
    smem = pl.BlockSpec(memory_space=pltpu.SMEM)
    nh = ATTN_HEADS_PER_STEP
    return pl.pallas_call(
        functools.partial(_attn_kernel, tq=tq, tk=tk, q_off=q_off, klen=klen, out_scale=out_scale, nh=nh),
        grid=(nb, B_HEADS // nh, t // tq),
        in_specs=[
            smem, smem,
            pl.BlockSpec((1, tq, nh * B_DV), lambda b, h, i: (b, i, h)),
            pl.BlockSpec((1, tkp, nh * B_DV), lambda b, h, i: (b, 0, h)),
            pl.BlockSpec((1, tkp, nh * B_DV), lambda b, h, i: (b, 0, h)),
            pl.BlockSpec((1, B_DV), lambda b, h, i: (0, 0)),
        ],
        out_specs=pl.BlockSpec((1, tq, nh * B_DV), lambda b, h, i: (b, i, h)),
        out_shape=jax.ShapeDtypeStruct((nb, t, B_WIDTH), F32),
        compiler_params=_cparams("parallel", "parallel", "arbitrary"),
        name="diff_attn",
    )(lam.reshape(1), slopes, q, k, v, norm_g.reshape(1, B_DV))


GDN_ROWS = 4


def _gdn_kernel(pc_ref, pab_ref, cst_ref, cw_ref, alog_ref, dtb_ref, ng_ref, s0_ref,
                yc_ref, sfin_ref, prev_scr, *, t_valid, bb):
    c = pl.program_id(1)

    @pl.when(c == 0)
    def _():
        prev_scr[...] = cst_ref[...]
        sfin_ref[...] = s0_ref[...]

    rows = [_gdn_chunk(bi, c, pc_ref, pab_ref, cw_ref, alog_ref, dtb_ref, ng_ref, yc_ref, sfin_ref, prev_scr, t_valid)
            for bi in range(bb)]
    while rows:
        rows = [r for r in rows if next(r, True) is None]


def _gdn_chunk(bi, c, pc_ref, pab_ref, cw_ref, alog_ref, dtb_ref, ng_ref, yc_ref, sfin_ref, prev_scr, t_valid):
    W = C_WIDTH
    qkv = pc_ref[bi, :, 0:C_QKV]
    z = pc_ref[bi, :, C_QKV:C_QKV + W]
    ext = jnp.concatenate([prev_scr[bi], qkv], axis=0)
    prev_scr[bi] = qkv[CHUNK - 8:, :]
    conv = ext[5:5 + CHUNK] * cw_ref[0:1, :]
    for j in range(1, CONV_W):
        conv = conv + ext[5 + j:5 + j + CHUNK] * cw_ref[j:j + 1, :]
    conv = conv * jax.nn.sigmoid(conv)
    q, k, v = conv[:, 0:W], conv[:, W:2 * W], conv[:, 2 * W:3 * W]

    head_ones = _group_ones(W, C_DK)
    qss = _dot_exact_rhs(q * q, head_ones)
    kss = _dot_exact_rhs(k * k, head_ones)
    pab = pab_ref[bi]
    g_cols = -jnp.exp(alog_ref[...]) * jax.nn.softplus(pab + dtb_ref[...])
    b_cols = jax.nn.sigmoid(pab)
    er = _iota((LANES, W), 0)
    eh = _iota((LANES, W), 1) // C_DK
    g = _dot_exact_rhs(g_cols, jnp.where(er == eh, 1.0, 0.0).astype(BF16))
    beta = _dot_exact_rhs(b_cols, jnp.where(er == eh + C_HEADS, 1.0, 0.0).astype(BF16))
    yield
    q = q * lax.rsqrt(qss + 1e-6) * (C_DK ** -0.5)
    k = k * lax.rsqrt(kss + 1e-6)
    if t_valid % CHUNK:
        valid = (c * CHUNK + _iota((CHUNK, W), 0)) < t_valid
        q, k, v = (jnp.where(valid, a, 0.0) for a in (q, k, v))
        g = jnp.where(valid, g, 0.0)
        beta = jnp.where(valid, beta, 0.0)

    tri = jnp.where(_iota((CHUNK, CHUNK), 0) >= _iota((CHUNK, CHUNK), 1), 1.0, 0.0).astype(BF16)
    gc = _dot_exact_lhs(tri, g)
    yield
    eg = jnp.exp(gc)
    glast = gc[CHUNK - 1:CHUNK, :]
    kdec = k * jnp.exp(glast - gc)
    qg = q * eg
    kb = k * beta
    kbe = kb * eg
    vb = v * beta

    R = C_HEADS * CHUNK
    same = (_iota((R, W), 0) // CHUNK) == (_iota((R, W), 1) // C_DK)
    ii = _iota((CHUNK, W), 0)
    jj = _iota((CHUNK, W), 1) % CHUNK

    same16 = jnp.where(same, 1.0, 0.0).astype(BF16)

    def bd(a):
        return jnp.where(same, jnp.concatenate([a] * C_HEADS, axis=0), 0.0)

    def bd16(a16):
        return jnp.concatenate([a16] * C_HEADS, axis=0) * same16

    def split_bd(a):
        hi = a.astype(BF16)
        lo = (a - hi.astype(F32)).astype(BF16)
        return hi, lo, bd16(hi), bd16(lo)

    def mm3(a_hi, a_lo, b_hi, b_lo):
        return _dot(a_hi, b_hi) + (_dot(a_hi, b_lo) + _dot(a_lo, b_hi))

    kx = bd16(k.astype(BF16))
    kk = _dot(kb.astype(BF16), kx, NT)
    qk = _dot(q.astype(BF16), kx, NT)
    pick = jnp.where(jj == 0, 1.0, 0.0).astype(BF16)
    gparts = _split(gc, 3)
    g_col = _dot(pick, bd16(gparts[2]), NT)
    g_col = g_col + _dot(pick, bd16(gparts[1]), NT)
    g_col = g_col + _dot(pick, bd16(gparts[0]), NT)
    yield
    lower = ii >= jj
    gamma = jnp.where(lower, jnp.exp(jnp.where(lower, gc - g_col, 0.0)), 0.0)
    nm = jnp.where(ii > jj, -(kk * gamma), 0.0)
    aqk = qk * gamma

    tinv = jnp.where(ii == jj, 1.0, 0.0) + nm
    p_hi, p_lo, pb_hi, pb_lo = split_bd(nm)
    for step in range(5):
        pw = mm3(p_hi, p_lo, pb_hi, pb_lo)
        yield
        p_hi, p_lo, pb_hi, pb_lo = split_bd(pw)
        t_hi, t_lo = _split(tinv, 2)
        tinv = tinv + mm3(t_hi, t_lo, pb_hi, pb_lo)
    _, _, vb_hi, vb_lo = split_bd(vb)
    _, _, ke_hi, ke_lo = split_bd(kbe)
    yield
    t_hi, t_lo = _split(tinv, 2)
    u = mm3(t_hi, t_lo, vb_hi, vb_lo)
    w = mm3(t_hi, t_lo, ke_hi, ke_lo)
    s = sfin_ref[bi]
    sb = s.astype(BF16)
    oq = _dot(qg.astype(BF16), sb)
    kdt = kdec.T.astype(BF16)
    yield
    v_new = u - _dot(w.astype(BF16), sb)
    yield
    vnb = v_new.astype(BF16)
    o = oq + _dot(aqk.astype(BF16), bd16(vnb))
    sfin_ref[bi] = s * jnp.exp(glast) + jnp.where(same, _dot(kdt, vnb), 0.0)
    yield
    ms = _dot_exact_rhs(o * o, head_ones) * (1.0 / C_DV)
    yield
    yc_ref[bi] = o * lax.rsqrt(ms + LN_EPS) * ng_ref[...] * (z * jax.nn.sigmoid(z))


def gdn(pc, pab, conv_state, s0, conv_w, a_log, dt_bias, norm_g, t_valid):
    nb, tp, _ = pc.shape
    cst = jnp.pad(conv_state, ((0, 0), (8 - (CONV_W - 1), 0), (0, 0)))
    cw = jnp.pad(conv_w, ((0, 8 - CONV_W), (0, 0)))
    pad_row = lambda a: jnp.pad(a.reshape(1, C_HEADS), ((0, 0), (0, LANES - C_HEADS)))
    eye = jnp.eye(C_HEADS, dtype=F32)
    s0_bd = jnp.einsum('bhde,hg->bhdge', s0, eye).reshape(nb, C_HEADS * C_DK, C_HEADS * C_DV)
    full = lambda b, c: (0, 0)
    bb = math.gcd(nb, GDN_ROWS)
    yc, sfin = pl.pallas_call(
        functools.partial(_gdn_kernel, t_valid=t_valid, bb=bb),
        grid=(nb // bb, tp // CHUNK),
        in_specs=[
            pl.BlockSpec((bb, CHUNK, 1024), lambda b, c: (b, c, 0)),
            pl.BlockSpec((bb, CHUNK, LANES), lambda b, c: (b, c, 0)),
            pl.BlockSpec((bb, 8, C_QKV), lambda b, c: (b, 0, 0)),
            pl.BlockSpec((8, C_QKV), full),
            pl.BlockSpec((1, LANES), full),
            pl.BlockSpec((1, LANES), full),
            pl.BlockSpec((1, C_WIDTH), full),
            pl.BlockSpec((bb, C_WIDTH, C_WIDTH), lambda b, c: (b, 0, 0)),
        ],
        out_specs=[
            pl.BlockSpec((bb, CHUNK, C_WIDTH), lambda b, c: (b, c, 0)),
            pl.BlockSpec((bb, C_WIDTH, C_WIDTH), lambda b, c: (b, 0, 0)),
        ],
        out_shape=[
            jax.ShapeDtypeStruct((nb, tp, C_WIDTH), F32),
            jax.ShapeDtypeStruct((nb, C_WIDTH, C_WIDTH), F32),
        ],
        scratch_shapes=[pltpu.VMEM((bb, 8, C_QKV), F32)],
        compiler_params=_cparams("parallel", "arbitrary"),
        name="gdn",
    )(pc, pab, cst, cw, pad_row(a_log), pad_row(dt_bias), jnp.tile(norm_g.reshape(1, C_DV), (1, C_HEADS)), s0_bd)
    s5 = sfin.reshape(nb, C_HEADS, C_DK, C_HEADS, C_DV)
    s_fin = jnp.stack([s5[:, h, :, h, :] for h in range(C_HEADS)], axis=1)
    return yc, s_fin


def _out_kernel(x_ref, ya_ref, yb_ref, yc_ref, mod_ref, w_ref, g_ref, b_ref, o_ref):
    y = _dot(ya_ref[...].astype(BF16), w_ref[0:256, :])
    y = y + _dot(yb_ref[...].astype(BF16), w_ref[256:768, :])
    y = y + _dot(yc_ref[...].astype(BF16), w_ref[768:1024, :])
    r = DN_ALPHA * x_ref[...] + mod_ref[0, 2:3, :] * y
    o_ref[...] = _ln(r) * g_ref[...] + b_ref[...]


def out_proj(x, ya, yb, yc, mod, w_out, ln_g, ln_b, t):
    n = x.shape[0]
    tm = min(512, t)
    per = t // tm
    row = lambda i: (i, 0)
    fixed = lambda i: (0, 0)
    return pl.pallas_call(
        _out_kernel,
        grid=(n // tm,),
        in_specs=[
            pl.BlockSpec((tm, D_MODEL), row),
            pl.BlockSpec((tm, A_WIDTH), row),
            pl.BlockSpec((tm, B_WIDTH), row),
            pl.BlockSpec((tm, C_WIDTH), row),
            pl.BlockSpec((1, 6, D_MODEL), lambda i: (i // per, 0, 0)),
            pl.BlockSpec((D_MODEL, D_MODEL), fixed),
            pl.BlockSpec((1, D_MODEL), fixed),
            pl.BlockSpec((1, D_MODEL), fixed),
        ],
        out_specs=pl.BlockSpec((tm, D_MODEL), row),
        out_shape=jax.ShapeDtypeStruct((n, D_MODEL), F32),
        compiler_params=_cparams("parallel"),
        name="out_proj",
    )(x, ya, yb, yc, mod, w_out, ln_g.reshape(1, D_MODEL), ln_b.reshape(1, D_MODEL))


def _round_robin(gens):
    gens = list(gens)
    while gens:
        gens = [g for g in gens if next(g, True) is None]


def _extract_top(vals, order, payload, out_s_ref, out_p_ref, base, lanes):
    big = 1e9
    for j in range(PEER_TOPK):
        m = jnp.max(vals, axis=0, keepdims=True)
        tied = jnp.where(vals == m, order, big)
        pos = jnp.min(tied, axis=0, keepdims=True)
        out_s_ref[base + j:base + j + 1, lanes] = m
        out_p_ref[base + j:base + j + 1, lanes] = jnp.max(jnp.where(tied == pos, payload, -1.0), axis=0, keepdims=True)
        vals = jnp.where(tied == pos, -jnp.inf, vals)
        yield


def _extract_top_keys(scores, out_s_ref, out_i_ref, base, lanes):
    half = PEER_NKEYS // 2
    a, b = scores[0:half], scores[half:PEER_NKEYS]
    ia = _iota((half, LANES), 0).astype(F32)
    ib = ia + float(half)
    swap = b > a
    top, rest = jnp.where(swap, b, a), jnp.where(swap, a, b)
    itop, irest = jnp.where(swap, ib, ia), jnp.where(swap, ia, ib)
    big = 1e9
    for j in range(PEER_TOPK):
        m = jnp.max(top, axis=0, keepdims=True)
        tied = jnp.where(top == m, itop, big)
        pos = jnp.min(tied, axis=0, keepdims=True)
        out_s_ref[base + j:base + j + 1, lanes] = m
        out_i_ref[base + j:base + j + 1, lanes] = pos
        hit = tied == pos
        top = jnp.where(hit, rest, top)
        itop = jnp.where(hit, irest, itop)
        rest = jnp.where(hit, -jnp.inf, rest)
        yield


def _mod_row(mod_ref, i):
    return mod_ref[0, i:i + 1, :] if len(mod_ref.shape) == 3 else mod_ref[0, i]


def _mod_spec(mod, t, tm, n):
    if tm > t:
        mod = jnp.repeat(mod, t, axis=0).reshape(n // tm, tm, 6, D_MODEL).transpose(0, 2, 1, 3)
        return mod, pl.BlockSpec((1, 6, tm, D_MODEL), lambda i, *_: (i, 0, 0, 0))
    per = t // tm
    return mod, pl.BlockSpec((1, 6, D_MODEL), lambda i, *_: (i // per, 0, 0))


def _peer_a_kernel(x_ref, mod_ref, wq_ref, keys_ref, h2_ref, ids_ref, gates_ref, ts_scr, ti_scr, bs_scr, id_scr):
    tm = x_ref.shape[0]
    h2 = _ln(x_ref[...]) * (1.0 + _mod_row(mod_ref, 4)) + _mod_row(mod_ref, 3)
    hb = h2.astype(BF16)
    h2_ref[...] = hb
    groups = [slice(g * LANES, (g + 1) * LANES) for g in range(tm // LANES)]
    for hp in range(2 * PEER_HEADS):
        qhp = _dot(hb, wq_ref[:, hp * LANES:(hp + 1) * LANES]).astype(BF16)
        st = _dot(keys_ref[hp], qhp, NT)
        _round_robin(_extract_top_keys(st[:, lanes], ts_scr, ti_scr, hp * PEER_TOPK, lanes) for lanes in groups)
    r8i = _iota((8, LANES), 0)
    r8 = r8i.astype(F32)
    for h in range(PEER_HEADS):
        b0 = 2 * h * PEER_TOPK
        b1 = b0 + PEER_TOPK
        rounds = []
        for lanes in groups:
            cand, ids, order = [], [], []

            def add(sa, sb, rows_a, lo, hi):
                va = ts_scr[b0 + sa[0]:b0 + sa[0] + sa[1], lanes]
                vb = ts_scr[b1 + sb[0]:b1 + sb[0] + sb[1], lanes]
                ia = ti_scr[b0 + sa[0]:b0 + sa[0] + sa[1], lanes]
                ib = ti_scr[b1 + sb[0]:b1 + sb[0] + sb[1], lanes]
                s = va + vb
                if (lo, hi) != (0, 8):
                    s = jnp.where(r8i >= lo, jnp.where(r8i < hi, s, -jnp.inf), -jnp.inf)
                cand.append(s)
                ids.append(ia * float(PEER_NKEYS) + ib)
                order.append((sa[0] + r8) * float(PEER_TOPK) + sb[0] if rows_a else sa[0] * PEER_TOPK + sb[0] + r8)

            add((0, 1), (0, 8), False, 0, 8)
            add((0, 1), (8, 8), False, 0, 8)
            add((1, 1), (0, 8), False, 0, 8)
            add((8, 8), (0, 1), True, 0, 8)
            for b in range(5):
                add((0, 8), (b, 1), True, 2, PEER_TOPK // (b + 1))
            rounds.append(_extract_top(jnp.concatenate(cand, axis=0), jnp.concatenate(order, axis=0),
                                       jnp.concatenate(ids, axis=0), bs_scr, id_scr, h * PEER_TOPK, lanes))
        _round_robin(rounds)
        bs =bs_scr[h * PEER_TOPK:(h + 1) * PEER_TOPK, :]
        e = jnp.exp(bs - bs[0:1, :])
        gates_ref[h * PEER_TOPK:(h + 1) * PEER_TOPK, :] = e / jnp.sum(e, axis=0, keepdims=True)
    ids_ref[...] = id_scr[...].astype(jnp.int32)


def peer_retrieve(x, mod, wq, keys, t):
    n = x.shape[0]
    tm = min(256, n)
    row = lambda i: (i, 0)
    col = lambda i: (0, i)
    mod, mod_spec = _mod_spec(mod, t, tm, n)
    return pl.pallas_call(
        _peer_a_kernel,
        grid=(n // tm,),
        in_specs=[
            pl.BlockSpec((tm, D_MODEL), row),
            mod_spec,
            pl.BlockSpec((D_MODEL, PEER_HEADS * PEER_DQ), lambda i: (0, 0)),
            pl.BlockSpec((2 * PEER_HEADS, PEER_NKEYS, PEER_DQ // 2), lambda i: (0, 0, 0)),
        ],
        out_specs=[pl.BlockSpec((tm, D_MODEL), row), pl.BlockSpec((PEER_SLOTS, tm), col), pl.BlockSpec((PEER_SLOTS, tm), col)],
        out_shape=[
            jax.ShapeDtypeStruct((n, D_MODEL), BF16),
            jax.ShapeDtypeStruct((PEER_SLOTS, n), jnp.int32),
            jax.ShapeDtypeStruct((PEER_SLOTS, n), F32),
        ],
        scratch_shapes=[
            pltpu.VMEM((2 * PEER_HEADS * PEER_TOPK, tm), F32),
            pltpu.VMEM((2 * PEER_HEADS * PEER_TOPK, tm), F32),
            pltpu.VMEM((PEER_SLOTS, tm), F32),
            pltpu.VMEM((PEER_SLOTS, tm), F32),
        ],
        compiler_params=_cparams("parallel"),
        name="peer_retrieve",
    )(x, mod, wq, keys)


PEER_CI = 16


GATE_TOK = 16


def _build_gates(ids_ref, gates_ref, base, gmat_scr, slot):
    sub = _iota((PEER_NKEYS, PEER_SLOTS), 0).astype(F32).astype(BF16)
    ids = ids_ref[pl.ds(base, GATE_TOK), :]
    gates = gates_ref[pl.ds(base, GATE_TOK), :].astype(BF16)
    key1 = (ids >> 7).astype(F32).astype(BF16)
    key2 = (ids & (PEER_NKEYS - 1)).astype(F32).astype(BF16)
    one = jnp.ones((PEER_NKEYS, PEER_SLOTS), BF16)
    zero = jnp.zeros((PEER_NKEYS, PEER_SLOTS), BF16)
    mats = []
    for r in range(GATE_TOK):
        a = jnp.where(sub == key1[r:r + 1, :], one, zero)
        b = jnp.where(sub == key2[r:r + 1, :], jnp.broadcast_to(gates[r:r + 1, :], zero.shape), zero)
        mats.append(_dot(a, b, NT))
    g = jnp.swapaxes(jnp.stack(mats, axis=0), 0, 1)
    gmat_scr[slot, :, pl.ds(base, GATE_TOK), :] = g.astype(BF16)


def _peer_b_kernel(h2_ref, ids_ref, gates_ref, ids_next_ref, gates_next_ref, ut_ref, v_ref, x_ref, mod_ref,
                   g_ref, b_ref, o_ref, gmat_scr, acc_scr, *, tm):
    i = pl.program_id(0)
    c = pl.program_id(1)
    nch = PEER_NKEYS // PEER_CI
    slot = i % 2

    @pl.when(jnp.logical_and(i == 0, c == 0))
    def _():
        def group(n, carry):
            _build_gates(ids_ref, gates_ref, pl.multiple_of(n * GATE_TOK, GATE_TOK), gmat_scr, 0)
            return carry

        lax.fori_loop(0, tm // GATE_TOK, group, 0)

    share = tm // nch
    for r in range(share // GATE_TOK):
        _build_gates(ids_next_ref, gates_next_ref, pl.multiple_of(c * share + r * GATE_TOK, GATE_TOK), gmat_scr, 1 - slot)

    act = _dot(h2_ref[...], ut_ref[...])
    ws = []
    for j in range(PEER_CI):
        gj = gmat_scr[slot, c * PEER_CI + j].astype(F32)
        ws.append((gj * jax.nn.gelu(act[:, j * LANES:(j + 1) * LANES])).astype(BF16))
    part = _dot(jnp.concatenate(ws, axis=1), v_ref[...])

    @pl.when(c == 0)
    def _():
        acc_scr[...] = part

    @pl.when(c > 0)
    def _():
        acc_scr[...] += part

    @pl.when(c == pl.num_programs(1) - 1)
    def _():
        r = DN_ALPHA * x_ref[...] + _mod_row(mod_ref, 5) * acc_scr[...]
        o_ref[...] = _ln(r) * g_ref[...] + b_ref[...]


def peer_mix(h2, ids, gates, ut, vv, x, mod, ln_g, ln_b, t):
    n = x.shape[0]
    tm = min(256, n)
    ec = PEER_CI * PEER_NKEYS
    row = lambda i, c: (i, 0)
    fixed = lambda i, c: (0, 0)
    last = n // tm - 1
    nxt = lambda i, c: (jnp.minimum(i + 1, last), 0)
    mod, mod_spec = _mod_spec(mod, t, tm, n)
    assert tm % (GATE_TOK * (PEER_NKEYS // PEER_CI)) == 0
    return pl.pallas_call(
        functools.partial(_peer_b_kernel, tm=tm),
        grid=(n // tm, PEER_NKEYS // PEER_CI),
        in_specs=[
            pl.BlockSpec((tm, D_MODEL), row),
            pl.BlockSpec((tm, PEER_SLOTS), row),
            pl.BlockSpec((tm, PEER_SLOTS), row),
            pl.BlockSpec((tm, PEER_SLOTS), nxt),
            pl.BlockSpec((tm, PEER_SLOTS), nxt),
            pl.BlockSpec((None, D_MODEL, ec), lambda i, c: (c, 0, 0)),
            pl.BlockSpec((ec, D_MODEL), lambda i, c: (c, 0)),
            pl.BlockSpec((tm, D_MODEL), row),
            mod_spec,
            pl.BlockSpec((1, D_MODEL), fixed),
            pl.BlockSpec((1, D_MODEL), fixed),
        ],
        out_specs=pl.BlockSpec((tm, D_MODEL), row),
        out_shape=jax.ShapeDtypeStruct((n, D_MODEL), F32),
        scratch_shapes=[pltpu.VMEM((2, PEER_NKEYS, tm, PEER_NKEYS), BF16), pltpu.VMEM((tm, D_MODEL), F32)],
        compiler_params=_cparams("arbitrary", "arbitrary"),
        name="peer_mix",
    )(h2, ids, gates, ids, gates, ut, vv, x, mod, ln_g.reshape(1, D_MODEL), ln_b.reshape(1, D_MODEL))


def _run_group(x3, mod_g, past_k, past_v, gdn_state, conv_state, wts):
    nb, t, _ = x3.shape
    n = nb * t
    cached = past_k is not None
    x = x3.reshape(n, D_MODEL)
    tp = -(-t // CHUNK) * CHUNK
    new_k, new_v, new_s, new_conv, new_sgu_v = [], [], [], [], []
    for l in range(DEPTH):
        mod = mod_g[l]
        pa, q, k, v, pc, pab = in_proj(x, mod, wts["w_main"][l], wts["w_ab"][l], t)
        ya, va = sgu(pa, wts["sgu_ln_g"][l], wts["sgu_ln_b"][l], wts["sgu_w"][l], wts["sgu_b"][l], t)
        lam_init = 0.8 - 0.6 * math.exp(-0.3 * l)
        lam = (jnp.exp(jnp.sum(wts["lam_q1"][l] * wts["lam_k1"][l]))
               - jnp.exp(jnp.sum(wts["lam_q2"][l] * wts["lam_k2"][l])) + lam_init)
        q3, k3, v3 = (a.reshape(nb, t, B_WIDTH) for a in (q, k, v))
        if cached:
            p_len = past_k.shape[2]
            tk = 256
            klen = p_len + t
            padk = -klen % tk
            kk = jnp.concatenate([past_k[l].reshape(nb, p_len, B_QK), k3, jnp.zeros((nb, padk, B_QK), F32)], axis=1)
            vv = jnp.concatenate([past_v[l].reshape(nb, p_len, B_WIDTH), v3, jnp.zeros((nb, padk, B_WIDTH), F32)], axis=1)
            yb = diff_attn(q3, kk, vv, lam, wts["diff_norm_g"][l], 1.0 - lam_init, q_off=p_len, klen=klen, tq=t, tk=tk)
        else:
            yb = diff_attn(q3, k3, v3, lam, wts["diff_norm_g"][l], 1.0 - lam_init, q_off=0, klen=t, tq=min(256, t), tk=256)
        pc3 = pc.reshape(nb, t, 1024)
        pab3 = pab.reshape(nb, t, LANES)
        if tp != t:
            pc3p = jnp.pad(pc3, ((0, 0), (0, tp - t), (0, 0)))
            pab3p = jnp.pad(pab3, ((0, 0), (0, tp - t), (0, 0)))
        else:
            pc3p, pab3p = pc3, pab3
        cst = conv_state[l] if cached else jnp.zeros((nb, CONV_W - 1, C_QKV), F32)
        s0 = gdn_state[l] if cached else jnp.zeros((nb, C_HEADS, C_DK, C_DV), F32)
        yc, s_fin = gdn(pc3p, pab3p, cst, s0, wts["conv_w"][l], wts["gdn_a_log"][l], wts["gdn_dt_bias"][l],
                        wts["gdn_norm_g"][l], t)
        yc = yc[:, :t].reshape(n, C_WIDTH)
        x = out_proj(x, ya, yb.reshape(n, B_WIDTH), yc, mod, wts["w_out"][l], wts["ln1_g"][l], wts["ln1_b"][l], t)
        h2, ids, gates = peer_retrieve(x, mod, wts["peer_wq"][l], wts["peer_keys"][l], t)
        x = peer_mix(h2, ids.T, gates.T, wts["expert_ut"][l], wts["expert_v"][l], x, mod,
                     wts["ln2_g"][l], wts["ln2_b"][l], t)
        new_k.append(k3.reshape(nb, t, B_HEADS, 2 * B_DK))
        new_v.append(v3.reshape(nb, t, B_HEADS, B_DV))
        new_s.append(s_fin)
        xp_tail = jnp.concatenate([cst, pc3[:, :, :C_QKV]], axis=1)[:, -(CONV_W - 1):] if t < CONV_W - 1 \
            else pc3[:, t - (CONV_W - 1):, :C_QKV]
        new_conv.append(xp_tail)
        new_sgu_v.append(va.reshape(nb, t, A_WIDTH))
    return (x.reshape(nb, t, D_MODEL), jnp.stack(new_k), jnp.stack(new_v), jnp.stack(new_s), jnp.stack(new_conv),
            jnp.stack(new_sgu_v))


def kernel(x_prompt, x_sample, cache_k, cache_v, state_gdn, state_conv, c_prompt, c_sample,
           w_ada, b_ada, w_in, sgu_ln_g, sgu_ln_b, sgu_w, sgu_b,
           lam_q1, lam_k1, lam_q2, lam_k2, diff_norm_g,
           conv_w, gdn_a_log, gdn_dt_bias, gdn_norm_g,
           w_out, ln1_g, ln1_b, peer_wq, peer_keys, expert_u, expert_v, ln2_g, ln2_b):
    nbp = x_prompt.shape[0]
    nbs = x_sample.shape[0]
    wts = dict(
        w_main=w_in[:, :, :P_MAIN].astype(BF16),
        w_ab=jnp.pad(w_in[:, :, P_MAIN:], ((0, 0), (0, 0), (0, LANES - (P_IN - P_MAIN)))).astype(BF16),
        sgu_ln_g=sgu_ln_g, sgu_ln_b=sgu_ln_b, sgu_w=sgu_w, sgu_b=sgu_b,
        lam_q1=lam_q1, lam_k1=lam_k1, lam_q2=lam_q2, lam_k2=lam_k2, diff_norm_g=diff_norm_g,
        conv_w=conv_w, gdn_a_log=gdn_a_log, gdn_dt_bias=gdn_dt_bias, gdn_norm_g=gdn_norm_g,
        w_out=w_out.astype(BF16), ln1_g=ln1_g, ln1_b=ln1_b,
        peer_wq=peer_wq.astype(BF16),
        peer_keys=peer_keys.reshape(DEPTH, 2 * PEER_HEADS, PEER_NKEYS, PEER_DQ // 2).astype(BF16),
        expert_ut=jnp.swapaxes(expert_u.astype(BF16).reshape(DEPTH, PEER_NKEYS // PEER_CI, PEER_CI * PEER_NKEYS, D_MODEL),
                               2, 3),
        expert_v=expert_v.astype(BF16),
        ln2_g=ln2_g, ln2_b=ln2_b,
    )
    mod_all = ada_mod(jnp.concatenate([c_prompt, c_sample], axis=0), w_ada, b_ada)
    mod_all = mod_all.reshape(DEPTH, nbp + nbs, 6, D_MODEL)
    y_p, k_p, v_p, s_p, conv_p, _ = _run_group(x_prompt, mod_all[:, :nbp], None, None, None, None, wts)
    y_s, k_s, v_s, s_s, conv_s, sgu_s = _run_group(x_sample, mod_all[:, nbp:], cache_k, cache_v, state_gdn, state_conv, wts)
    return (y_p, y_s, k_p, v_p, s_p, conv_p, k_s, v_s, s_s, conv_s, sgu_s)
```

```python
import functools
import math

import jax
import jax.numpy as jnp
from jax import lax
from jax.experimental import pallas as pl
from jax.experimental.pallas import tpu as pltpu

F32 = jnp.float32
BF16 = jnp.bfloat16

D_MODEL = 1024
DEPTH = 4
CHUNK = 64
A_GROUPS = 4
A_GDIM = 64
A_WIDTH = 256
SGU_LEN = 128
B_HEADS = 4
B_DK = 64
B_DV = 128
B_WIDTH = 512
B_QK = 512
C_HEADS = 4
C_DK = 64
C_DV = 64
C_WIDTH = 256
CONV_W = 4
C_QKV = 768
P_MAIN = 3072
P_IN = 3080
PEER_HEADS = 8
PEER_NKEYS = 128
PEER_DQ = 256
PEER_TOPK = 16
PEER_SLOTS = PEER_HEADS * PEER_TOPK
DN_ALPHA = (2 * DEPTH) ** 0.25
LN_EPS = 1e-5
NEG = -1e30

LANES = 128
VMEM_LIMIT = 56 * 1024 * 1024

NN = (((1,), (0,)), ((), ()))
NT = (((1,), (1,)), ((), ()))


def _cparams(*sem):
    return pltpu.CompilerParams(dimension_semantics=sem, vmem_limit_bytes=VMEM_LIMIT)


def _dot(a, b, dims=NN):
    return lax.dot_general(a, b, dims, preferred_element_type=F32)


def _split(x, n):
    parts = []
    r = x
    for _ in range(n - 1):
        p = r.astype(BF16)
        parts.append(p)
        r = r - p.astype(F32)
    parts.append(r.astype(BF16))
    return parts


def _dot3(a, b, dims=NN):
    ah, al = _split(a, 2)
    bh, bl = _split(b, 2)
    return _dot(ah, bh, dims) + (_dot(ah, bl, dims) + _dot(al, bh, dims))


def _dot_exact_rhs(x, m_bf16, n=3, dims=NN):
    parts = _split(x, n)
    acc = _dot(parts[-1], m_bf16, dims)
    for p in parts[-2::-1]:
        acc = acc + _dot(p, m_bf16, dims)
    return acc


def _dot_exact_lhs(m_bf16, x, n=3, dims=NN):
    parts = _split(x, n)
    acc = _dot(m_bf16, parts[-1], dims)
    for p in parts[-2::-1]:
        acc = acc + _dot(m_bf16, p, dims)
    return acc


def _ln(x):
    mu = jnp.mean(x, axis=-1, keepdims=True)
    d = x - mu
    var = jnp.mean(d * d, axis=-1, keepdims=True)
    return d * lax.rsqrt(var + LN_EPS)


def _iota(shape, dim):
    return lax.broadcasted_iota(jnp.int32, shape, dim)


def _ada_kernel(c_ref, w_ref, b_ref, o_ref):
    c = c_ref[...]
    s = c * jax.nn.sigmoid(c)
    o_ref[0] = _dot3(s, w_ref[0]) + b_ref[0]


def ada_mod(c_all, w_ada, b_ada):
    nbt = c_all.shape[0]
    tn = 1536
    return pl.pallas_call(
        _ada_kernel,
        grid=(DEPTH, 6 * D_MODEL // tn),
        in_specs=[
            pl.BlockSpec((nbt, D_MODEL), lambda l, j: (0, 0)),
            pl.BlockSpec((1, D_MODEL, tn), lambda l, j: (l, 0, j)),
            pl.BlockSpec((1, 1, tn), lambda l, j: (l, 0, j)),
        ],
        out_specs=pl.BlockSpec((1, nbt, tn), lambda l, j: (l, 0, j)),
        out_shape=jax.ShapeDtypeStruct((DEPTH, nbt, 6 * D_MODEL), F32),
        compiler_params=_cparams("parallel", "parallel"),
        name="ada_mod",
    )(c_all, w_ada, b_ada.reshape(DEPTH, 1, 6 * D_MODEL))


def _in_kernel(x_ref, mod_ref, w_ref, wab_ref, pa_ref, q_ref, k_ref, v_ref, pc_ref, pab_ref):
    h = _ln(x_ref[...]) * (1.0 + mod_ref[0, 1:2, :]) + mod_ref[0, 0:1, :]
    hb = h.astype(BF16)
    pa_ref[...] = _dot(hb, w_ref[:, 0:512])
    q_ref[...] = _dot(hb, w_ref[:, 512:1024])
    k_ref[...] = _dot(hb, w_ref[:, 1024:1536])
    v_ref[...] = _dot(hb, w_ref[:, 1536:2048])
    pc_ref[...] = _dot(hb, w_ref[:, 2048:3072])
    pab_ref[...] = _dot(hb, wab_ref[...])


def in_proj(x, mod, w_main, w_ab, t):
    n = x.shape[0]
    tm = min(512, t)
    per = t // tm
    row = lambda i: (i, 0)
    fixed = lambda i: (0, 0)
    sds = lambda w: jax.ShapeDtypeStruct((n, w), F32)
    return pl.pallas_call(
        _in_kernel,
        grid=(n // tm,),
        in_specs=[
            pl.BlockSpec((tm, D_MODEL), row),
            pl.BlockSpec((1, 6, D_MODEL), lambda i: (i // per, 0, 0)),
            pl.BlockSpec((D_MODEL, P_MAIN), fixed),
            pl.BlockSpec((D_MODEL, LANES), fixed),
        ],
        out_specs=[pl.BlockSpec((tm, 512), row)] * 4 + [pl.BlockSpec((tm, 1024), row), pl.BlockSpec((tm, LANES), row)],
        out_shape=[sds(512)] * 4 + [sds(1024), sds(LANES)],
        compiler_params=_cparams("parallel"),
        name="in_proj",
    )(x, mod, w_main, w_ab)


def _group_ones(width, gdim):
    r = _iota((width, width), 0) // gdim
    c = _iota((width, width), 1) // gdim
    return jnp.where(r == c, 1.0, 0.0).astype(BF16)


def _sgu_kernel(pa_ref, g_ref, b_ref, w_ref, bs_ref, ya_ref, va_ref, *, blk, nblk):
    pa = pa_ref[...]
    u = jax.nn.gelu(pa[:, :A_WIDTH])
    v = jax.nn.gelu(pa[:, A_WIDTH:])
    ones = _group_ones(A_WIDTH, A_GDIM)
    mean = _dot_exact_rhs(v, ones) * (1.0 / A_GDIM)
    d = v - mean
    var = _dot_exact_rhs(d * d, ones) * (1.0 / A_GDIM)
    vn = d * lax.rsqrt(var + LN_EPS) * g_ref[...] + b_ref[...]
    va_ref[...] = vn
    ri = _iota((blk, blk), 0) // CHUNK
    ci = _iota((blk, blk), 1) // CHUNK
    lane_group = _iota((blk, A_WIDTH), 1) // A_GDIM
    wm = [jnp.where(ci <= ri, w_ref[g], 0.0).astype(BF16) for g in range(A_GROUPS)]
    for r in range(nblk):
        vb = vn[r * blk:(r + 1) * blk, :].astype(BF16)
        s = jnp.zeros((blk, A_WIDTH), F32)
        for g in range(A_GROUPS):
            s = jnp.where(lane_group == g, _dot(wm[g], vb), s)
        ya_ref[r * blk:(r + 1) * blk, :] = u[r * blk:(r + 1) * blk, :] * (s + bs_ref[...])


def sgu(pa, ln_g, ln_b, w_s, b_s, t):
    n = pa.shape[0]
    blk = min(SGU_LEN, t)
    assert t % blk == 0
    tm = min(512, n)
    w = w_s[:, :blk, :blk]
    bs = jnp.repeat(b_s.T[:blk], A_GDIM, axis=1)
    row = lambda i: (i, 0)
    return pl.pallas_call(
        functools.partial(_sgu_kernel, blk=blk, nblk=tm // blk),
        grid=(n // tm,),
        in_specs=[
            pl.BlockSpec((tm, 2 * A_WIDTH), row),
            pl.BlockSpec((1, A_WIDTH), lambda i: (0, 0)),
            pl.BlockSpec((1, A_WIDTH), lambda i: (0, 0)),
            pl.BlockSpec((A_GROUPS, blk, blk), lambda i: (0, 0, 0)),
            pl.BlockSpec((blk, A_WIDTH), lambda i: (0, 0)),
        ],
        out_specs=[pl.BlockSpec((tm, A_WIDTH), row)] * 2,
        out_shape=[jax.ShapeDtypeStruct((n, A_WIDTH), F32)] * 2,
        compiler_params=_cparams("parallel"),
        name="sgu",
    )(pa, ln_g.reshape(1, A_WIDTH), ln_b.reshape(1, A_WIDTH), w, bs)


ATTN_HEADS_PER_STEP = 2


def _attn_kernel(lam_ref, slope_ref, q_ref, k_ref, v_ref, g_ref, o_ref, *, tq, tk, q_off, klen, out_scale, nh):
    hg = pl.program_id(1)
    qi = pl.program_id(2)
    rows = 2 * tq
    log2e = 1.4426950408889634
    lane = _iota((tq, B_DV), 1)
    q_lo = q_off + qi * tq
    qpos = q_lo + (_iota((rows, tk), 0) & (tq - 1))
    col = _iota((rows, tk), 1)
    colf = _iota((1, tk), 1).astype(F32)
    limit = jnp.minimum(klen, ((q_lo + tq - 1) // CHUNK + 1) * CHUNK)
    nkb = (limit + tk - 1) // tk
    nfull = jnp.minimum(q_lo, klen) // tk
    heads = range(nh)
    hl = [slice(e * B_DV, (e + 1) * B_DV) for e in heads]
    slope2 = [slope_ref[hg * nh + e] * log2e for e in heads]
    qq = []
    for e in heads:
        q = q_ref[0, :, hl[e]] * (B_DK ** -0.5 * log2e)
        qq.append(jnp.concatenate([jnp.where(lane < B_DK, q, 0.0), jnp.where(lane >= B_DK, q, 0.0)],
                                  axis=0).astype(BF16))

    def scores(e, kb):
        start = pl.multiple_of(kb * tk, tk)
        return _dot(qq[e], k_ref[0, pl.ds(start, tk), hl[e]].astype(BF16), NT)

    def update(e, kb, s, m, l, acc):
        start = pl.multiple_of(kb * tk, tk)
        vblk = v_ref[0, pl.ds(start, tk), hl[e]].astype(BF16)
        m_new = jnp.maximum(m, jnp.max(s, axis=1, keepdims=True))
        alpha = jnp.exp2(m - m_new)
        p = jnp.exp2(s - m_new)
        l = alpha * l + jnp.sum(p, axis=1, keepdims=True)
        acc = alpha * acc + _dot(p.astype(BF16), vblk)
        return m_new, l, acc

    def masked(e, kb, s):
        kpos = kb * tk + col
        s = s + slope2[e] * (qpos - jnp.abs(qpos - kpos)).astype(F32)
        kchunk = jnp.where(kpos < klen, kpos // CHUNK, jnp.int32(1 << 30))
        return jnp.where(kchunk <= (qpos // CHUNK), s, NEG)

    def full_body(kb, carry):
        ss = [scores(e, kb) + slope2[e] * ((kb * tk).astype(F32) + colf) for e in heads]
        return tuple(update(e, kb, ss[e], *carry[e]) for e in heads)

    def tail_body(kb, carry):
        ss = [masked(e, kb, scores(e, kb)) for e in heads]
        return tuple(update(e, kb, ss[e], *carry[e]) for e in heads)

    init = tuple((jnp.full((rows, 1), NEG, F32), jnp.zeros((rows, 1), F32), jnp.zeros((rows, B_DV), F32))
                 for e in heads)
    carry = lax.fori_loop(0, nfull, full_body, init)
    carry = lax.fori_loop(nfull, nkb, tail_body, carry)
    for e in heads:
        m, l, acc = carry[e]
        o = acc / l
        ob = o[:tq] - lam_ref[0] * o[tq:]
        ms = jnp.mean(ob * ob, axis=-1, keepdims=True)
        o_ref[0, :, hl[e]] = ob * lax.rsqrt(ms + LN_EPS) * g_ref[...] * out_scale


def diff_attn(q, k, v, lam, norm_g, out_scale, *, q_off, klen, tq, tk):
    nb, t, _ = q.shape
    tkp = k.shape[1]
    assert klen >= q_off + t and tkp % tk == 0 and tkp >= klen and t % tq == 0
    slopes = jnp.exp2(-(8.0 / B_HEADS) * jnp.arange(1, B_HEADS + 1, dtype=F32))
    smem = pl.BlockSpec(memory_space=pltpu.SMEM)
    nh = ATTN_HEADS_PER_STEP
    return pl.pallas_call(
        functools.partial(_attn_kernel, tq=tq, tk=tk, q_off=q_off, klen=klen, out_scale=out_scale, nh=nh),
        grid=(nb, B_HEADS // nh, t // tq),
        in_specs=[
            smem, smem,
            pl.BlockSpec((1, tq, nh * B_DV), lambda b, h, i: (b, i, h)),
            pl.BlockSpec((1, tkp, nh * B_DV), lambda b, h, i: (b, 0, h)),
            pl.BlockSpec((1, tkp, nh * B_DV), lambda b, h, i: (b, 0, h)),
            pl.BlockSpec((1, B_DV), lambda b, h, i: (0, 0)),
        ],
        out_specs=pl.BlockSpec((1, tq, nh * B_DV), lambda b, h, i: (b, i, h)),
        out_shape=jax.ShapeDtypeStruct((nb, t, B_WIDTH), F32),
        compiler_params=_cparams("parallel", "parallel", "arbitrary"),
        name="diff_attn",
    )(lam.reshape(1), slopes, q, k, v, norm_g.reshape(1, B_DV))


GDN_ROWS = 4


def _gdn_kernel(pc_ref, pab_ref, cst_ref, cw_ref, alog_ref, dtb_ref, ng_ref, s0_ref,
                yc_ref, sfin_ref, prev_scr, *, t_valid, bb):
    c = pl.program_id(1)

    @pl.when(c == 0)
    def _():
        prev_scr[...] = cst_ref[...]
        sfin_ref[...] = s0_ref[...]

    rows = [_gdn_chunk(bi, c, pc_ref, pab_ref, cw_ref, alog_ref, dtb_ref, ng_ref, yc_ref, sfin_ref, prev_scr, t_valid)
            for bi in range(bb)]
    while rows:
        rows = [r for r in rows if next(r, True) is None]


def _gdn_chunk(bi, c, pc_ref, pab_ref, cw_ref, alog_ref, dtb_ref, ng_ref, yc_ref, sfin_ref, prev_scr, t_valid):
    W = C_WIDTH
    qkv = pc_ref[bi, :, 0:C_QKV]
    z = pc_ref[bi, :, C_QKV:C_QKV + W]
    ext = jnp.concatenate([prev_scr[bi], qkv], axis=0)
    prev_scr[bi] = qkv[CHUNK - 8:, :]
    conv = ext[5:5 + CHUNK] * cw_ref[0:1, :]
    for j in range(1, CONV_W):
        conv = conv + ext[5 + j:5 + j + CHUNK] * cw_ref[j:j + 1, :]
    conv = conv * jax.nn.sigmoid(conv)
    q, k, v = conv[:, 0:W], conv[:, W:2 * W], conv[:, 2 * W:3 * W]

    head_ones = _group_ones(W, C_DK)
    qss = _dot_exact_rhs(q * q, head_ones)
    kss = _dot_exact_rhs(k * k, head_ones)
    pab = pab_ref[bi]
    g_cols = -jnp.exp(alog_ref[...]) * jax.nn.softplus(pab + dtb_ref[...])
    b_cols = jax.nn.sigmoid(pab)
    er = _iota((LANES, W), 0)
    eh = _iota((LANES, W), 1) // C_DK
    g = _dot_exact_rhs(g_cols, jnp.where(er == eh, 1.0, 0.0).astype(BF16))
    beta = _dot_exact_rhs(b_cols, jnp.where(er == eh + C_HEADS, 1.0, 0.0).astype(BF16))
    yield
    q = q * lax.rsqrt(qss + 1e-6) * (C_DK ** -0.5)
    k = k * lax.rsqrt(kss + 1e-6)
    if t_valid % CHUNK:
        valid = (c * CHUNK + _iota((CHUNK, W), 0)) < t_valid
        q, k, v = (jnp.where(valid, a, 0.0) for a in (q, k, v))
        g = jnp.where(valid, g, 0.0)
        beta = jnp.where(valid, beta, 0.0)

    tri = jnp.where(_iota((CHUNK, CHUNK), 0) >= _iota((CHUNK, CHUNK), 1), 1.0, 0.0).astype(BF16)
    gc = _dot_exact_lhs(tri, g)
    yield
    eg = jnp.exp(gc)
    glast = gc[CHUNK - 1:CHUNK, :]
    kdec = k * jnp.exp(glast - gc)
    qg = q * eg
    kb = k * beta
    kbe = kb * eg
    vb = v * beta

    R = C_HEADS * CHUNK
    same = (_iota((R, W), 0) // CHUNK) == (_iota((R, W), 1) // C_DK)
    ii = _iota((CHUNK, W), 0)
    jj = _iota((CHUNK, W), 1) % CHUNK

    same16 = jnp.where(same, 1.0, 0.0).astype(BF16)

    def bd(a):
        return jnp.where(same, jnp.concatenate([a] * C_HEADS, axis=0), 0.0)

    def bd16(a16):
        return jnp.concatenate([a16] * C_HEADS, axis=0) * same16

    def split_bd(a):
        hi = a.astype(BF16)
        lo = (a - hi.astype(F32)).astype(BF16)
        return hi, lo, bd16(hi), bd16(lo)

    def mm3(a_hi, a_lo, b_hi, b_lo):
        return _dot(a_hi, b_hi) + (_dot(a_hi, b_lo) + _dot(a_lo, b_hi))

    kx = bd16(k.astype(BF16))
    kk = _dot(kb.astype(BF16), kx, NT)
    qk = _dot(q.astype(BF16), kx, NT)
    pick = jnp.where(jj == 0, 1.0, 0.0).astype(BF16)
    gparts = _split(gc, 3)
    g_col = _dot(pick, bd16(gparts[2]), NT)
    g_col = g_col + _dot(pick, bd16(gparts[1]), NT)
    g_col = g_col + _dot(pick, bd16(gparts[0]), NT)
    yield
    lower = ii >= jj
    gamma = jnp.where(lower, jnp.exp(jnp.where(lower, gc - g_col, 0.0)), 0.0)
    nm = jnp.where(ii > jj, -(kk * gamma), 0.0)
    aqk = qk * gamma

    tinv = jnp.where(ii == jj, 1.0, 0.0) + nm
    p_hi, p_lo, pb_hi, pb_lo = split_bd(nm)
    for step in range(5):
        pw = mm3(p_hi, p_lo, pb_hi, pb_lo)
        yield
        p_hi, p_lo, pb_hi, pb_lo = split_bd(pw)
        t_hi, t_lo = _split(tinv, 2)
        tinv = tinv + mm3(t_hi, t_lo, pb_hi, pb_lo)
    _, _, vb_hi, vb_lo = split_bd(vb)
    _, _, ke_hi, ke_lo = split_bd(kbe)
    yield
    t_hi, t_lo = _split(tinv, 2)
    u = mm3(t_hi, t_lo, vb_hi, vb_lo)
    w = mm3(t_hi, t_lo, ke_hi, ke_lo)
    s = sfin_ref[bi]
    sb = s.astype(BF16)
    oq = _dot(qg.astype(BF16), sb)
    kdt = kdec.T.astype(BF16)
    yield
    v_new = u - _dot(w.astype(BF16), sb)
    yield
    vnb = v_new.astype(BF16)
    o = oq + _dot(aqk.astype(BF16), bd16(vnb))
    sfin_ref[bi] = s * jnp.exp(glast) + jnp.where(same, _dot(kdt, vnb), 0.0)
    yield
    ms = _dot_exact_rhs(o * o, head_ones) * (1.0 / C_DV)
    yield
    yc_ref[bi] = o * lax.rsqrt(ms + LN_EPS) * ng_ref[...] * (z * jax.nn.sigmoid(z))


def gdn(pc, pab, conv_state, s0, conv_w, a_log, dt_bias, norm_g, t_valid):
    nb, tp, _ = pc.shape
    cst = jnp.pad(conv_state, ((0, 0), (8 - (CONV_W - 1), 0), (0, 0)))
    cw = jnp.pad(conv_w, ((0, 8 - CONV_W), (0, 0)))
    pad_row = lambda a: jnp.pad(a.reshape(1, C_HEADS), ((0, 0), (0, LANES - C_HEADS)))
    eye = jnp.eye(C_HEADS, dtype=F32)
    s0_bd = jnp.einsum('bhde,hg->bhdge', s0, eye).reshape(nb, C_HEADS * C_DK, C_HEADS * C_DV)
    full = lambda b, c: (0, 0)
    bb = math.gcd(nb, GDN_ROWS)
    yc, sfin = pl.pallas_call(
        functools.partial(_gdn_kernel, t_valid=t_valid, bb=bb),
        grid=(nb // bb, tp // CHUNK),
        in_specs=[
            pl.BlockSpec((bb, CHUNK, 1024), lambda b, c: (b, c, 0)),
            pl.BlockSpec((bb, CHUNK, LANES), lambda b, c: (b, c, 0)),
            pl.BlockSpec((bb, 8, C_QKV), lambda b, c: (b, 0, 0)),
            pl.BlockSpec((8, C_QKV), full),
            pl.BlockSpec((1, LANES), full),
            pl.BlockSpec((1, LANES), full),
            pl.BlockSpec((1, C_WIDTH), full),
            pl.BlockSpec((bb, C_WIDTH, C_WIDTH), lambda b, c: (b, 0, 0)),
        ],
        out_specs=[
            pl.BlockSpec((bb, CHUNK, C_WIDTH), lambda b, c: (b, c, 0)),
            pl.BlockSpec((bb, C_WIDTH, C_WIDTH), lambda b, c: (b, 0, 0)),
        ],
        out_shape=[
            jax.ShapeDtypeStruct((nb, tp, C_WIDTH), F32),
            jax.ShapeDtypeStruct((nb, C_WIDTH, C_WIDTH), F32),
        ],
        scratch_shapes=[pltpu.VMEM((bb, 8, C_QKV), F32)],
        compiler_params=_cparams("parallel", "arbitrary"),
        name="gdn",
    )(pc, pab, cst, cw, pad_row(a_log), pad_row(dt_bias), jnp.tile(norm_g.reshape(1, C_DV), (1, C_HEADS)), s0_bd)
    s5 = sfin.reshape(nb, C_HEADS, C_DK, C_HEADS, C_DV)
    s_fin = jnp.stack([s5[:, h, :, h, :] for h in range(C_HEADS)], axis=1)
    return yc, s_fin


def _out_kernel(x_ref, ya_ref, yb_ref, yc_ref, mod_ref, w_ref, g_ref, b_ref, o_ref):
    y = _dot(ya_ref[...].astype(BF16), w_ref[0:256, :])
    y = y + _dot(yb_ref[...].astype(BF16), w_ref[256:768, :])
    y = y + _dot(yc_ref[...].astype(BF16), w_ref[768:1024, :])
    r = DN_ALPHA * x_ref[...] + mod_ref[0, 2:3, :] * y
    o_ref[...] = _ln(r) * g_ref[...] + b_ref[...]


def out_proj(x, ya, yb, yc, mod, w_out, ln_g, ln_b, t):
    n = x.shape[0]
    tm = min(512, t)
    per = t // tm
    row = lambda i: (i, 0)
    fixed = lambda i: (0, 0)
    return pl.pallas_call(
        _out_kernel,
        grid=(n // tm,),
        in_specs=[
            pl.BlockSpec((tm, D_MODEL), row),
            pl.BlockSpec((tm, A_WIDTH), row),
            pl.BlockSpec((tm, B_WIDTH), row),
            pl.BlockSpec((tm, C_WIDTH), row),
            pl.BlockSpec((1, 6, D_MODEL), lambda i: (i // per, 0, 0)),
            pl.BlockSpec((D_MODEL, D_MODEL), fixed),
            pl.BlockSpec((1, D_MODEL), fixed),
            pl.BlockSpec((1, D_MODEL), fixed),
        ],
        out_specs=pl.BlockSpec((tm, D_MODEL), row),
        out_shape=jax.ShapeDtypeStruct((n, D_MODEL), F32),
        compiler_params=_cparams("parallel"),
        name="out_proj",
    )(x, ya, yb, yc, mod, w_out, ln_g.reshape(1, D_MODEL), ln_b.reshape(1, D_MODEL))


def _round_robin(gens):
    gens = list(gens)
    while gens:
        gens = [g for g in gens if next(g, True) is None]


def _extract_top(vals, order, payload, out_s_ref, out_p_ref, base, lanes):
    big = 1e9
    for j in range(PEER_TOPK):
        m = jnp.max(vals, axis=0, keepdims=True)
        tied = jnp.where(vals == m, order, big)
        pos = jnp.min(tied, axis=0, keepdims=True)
        out_s_ref[base + j:base + j + 1, lanes] = m
        out_p_ref[base + j:base + j + 1, lanes] = jnp.max(jnp.where(tied == pos, payload, -1.0), axis=0, keepdims=True)
        vals = jnp.where(tied == pos, -jnp.inf, vals)
        yield


def _extract_top_keys(scores, out_s_ref, out_i_ref, base, lanes):
    half = PEER_NKEYS // 2
    a, b = scores[0:half], scores[half:PEER_NKEYS]
    ia = _iota((half, LANES), 0).astype(F32)
    ib = ia + float(half)
    swap = b > a
    top, rest = jnp.where(swap, b, a), jnp.where(swap, a, b)
    itop, irest = jnp.where(swap, ib, ia), jnp.where(swap, ia, ib)
    big = 1e9
    for j in range(PEER_TOPK):
        m = jnp.max(top, axis=0, keepdims=True)
        tied = jnp.where(top == m, itop, big)
        pos = jnp.min(tied, axis=0, keepdims=True)
        out_s_ref[base + j:base + j + 1, lanes] = m
        out_i_ref[base + j:base + j + 1, lanes] = pos
        hit = tied == pos
        top = jnp.where(hit, rest, top)
        itop = jnp.where(hit, irest, itop)
        rest = jnp.where(hit, -jnp.inf, rest)
        yield


def _mod_row(mod_ref, i):
    return mod_ref[0, i:i + 1, :] if len(mod_ref.shape) == 3 else mod_ref[0, i]


def _mod_spec(mod, t, tm, n):
    if tm > t:
        mod = jnp.repeat(mod, t, axis=0).reshape(n // tm, tm, 6, D_MODEL).transpose(0, 2, 1, 3)
        return mod, pl.BlockSpec((1, 6, tm, D_MODEL), lambda i, *_: (i, 0, 0, 0))
    per = t // tm
    return mod, pl.BlockSpec((1, 6, D_MODEL), lambda i, *_: (i // per, 0, 0))


def _peer_a_kernel(x_ref, mod_ref, wq_ref, keys_ref, h2_ref, ids_ref, gates_ref, ts_scr, ti_scr, bs_scr, id_scr):
    tm = x_ref.shape[0]
    h2 = _ln(x_ref[...]) * (1.0 + _mod_row(mod_ref, 4)) + _mod_row(mod_ref, 3)
    hb = h2.astype(BF16)
    h2_ref[...] = hb
    groups = [slice(g * LANES, (g + 1) * LANES) for g in range(tm // LANES)]
    for hp in range(2 * PEER_HEADS):
        qhp = _dot(hb, wq_ref[:, hp * LANES:(hp + 1) * LANES]).astype(BF16)
        st = _dot(keys_ref[hp], qhp, NT)
        _round_robin(_extract_top_keys(st[:, lanes], ts_scr, ti_scr, hp * PEER_TOPK, lanes) for lanes in groups)
    r8i = _iota((8, LANES), 0)
    r8 = r8i.astype(F32)
    for h in range(PEER_HEADS):
        b0 = 2 * h * PEER_TOPK
        b1 = b0 + PEER_TOPK
        rounds = []
        for lanes in groups:
            cand, ids, order = [], [], []

            def add(sa, sb, rows_a, lo, hi):
                va = ts_scr[b0 + sa[0]:b0 + sa[0] + sa[1], lanes]
                vb = ts_scr[b1 + sb[0]:b1 + sb[0] + sb[1], lanes]
                ia = ti_scr[b0 + sa[0]:b0 + sa[0] + sa[1], lanes]
                ib = ti_scr[b1 + sb[0]:b1 + sb[0] + sb[1], lanes]
                s = va + vb
                if (lo, hi) != (0, 8):
                    s = jnp.where(r8i >= lo, jnp.where(r8i < hi, s, -jnp.inf), -jnp.inf)
                cand.append(s)
                ids.append(ia * float(PEER_NKEYS) + ib)
                order.append((sa[0] + r8) * float(PEER_TOPK) + sb[0] if rows_a else sa[0] * PEER_TOPK + sb[0] + r8)

            add((0, 1), (0, 8), False, 0, 8)
            add((0, 1), (8, 8), False, 0, 8)
            add((1, 1), (0, 8), False, 0, 8)
            add((8, 8), (0, 1), True, 0, 8)
            for b in range(5):
                add((0, 8), (b, 1), True, 2, PEER_TOPK // (b + 1))
            rounds.append(_extract_top(jnp.concatenate(cand, axis=0), jnp.concatenate(order, axis=0),
                                       jnp.concatenate(ids, axis=0), bs_scr, id_scr, h * PEER_TOPK, lanes))
        _round_robin(rounds)
        bs =bs_scr[h * PEER_TOPK:(h + 1) * PEER_TOPK, :]
        e = jnp.exp(bs - bs[0:1, :])
        gates_ref[h * PEER_TOPK:(h + 1) * PEER_TOPK, :] = e / jnp.sum(e, axis=0, keepdims=True)
    ids_ref[...] = id_scr[...].astype(jnp.int32)


def peer_retrieve(x, mod, wq, keys, t):
    n = x.shape[0]
    tm = min(256, n)
    row = lambda i: (i, 0)
    col = lambda i: (0, i)
    mod, mod_spec = _mod_spec(mod, t, tm, n)
    return pl.pallas_call(
        _peer_a_kernel,
        grid=(n // tm,),
        in_specs=[
            pl.BlockSpec((tm, D_MODEL), row),
            mod_spec,
            pl.BlockSpec((D_MODEL, PEER_HEADS * PEER_DQ), lambda i: (0, 0)),
            pl.BlockSpec((2 * PEER_HEADS, PEER_NKEYS, PEER_DQ // 2), lambda i: (0, 0, 0)),
        ],
        out_specs=[pl.BlockSpec((tm, D_MODEL), row), pl.BlockSpec((PEER_SLOTS, tm), col), pl.BlockSpec((PEER_SLOTS, tm), col)],
        out_shape=[
            jax.ShapeDtypeStruct((n, D_MODEL), BF16),
            jax.ShapeDtypeStruct((PEER_SLOTS, n), jnp.int32),
            jax.ShapeDtypeStruct((PEER_SLOTS, n), F32),
        ],
        scratch_shapes=[
            pltpu.VMEM((2 * PEER_HEADS * PEER_TOPK, tm), F32),
            pltpu.VMEM((2 * PEER_HEADS * PEER_TOPK, tm), F32),
            pltpu.VMEM((PEER_SLOTS, tm), F32),
            pltpu.VMEM((PEER_SLOTS, tm), F32),
        ],
        compiler_params=_cparams("parallel"),
        name="peer_retrieve",
    )(x, mod, wq, keys)


PEER_CI = 16


GATE_TOK = 16


def _build_gates(ids_ref, gates_ref, base, gmat_scr, slot):
    sub = _iota((PEER_NKEYS, PEER_SLOTS), 0).astype(F32).astype(BF16)
    ids = ids_ref[pl.ds(base, GATE_TOK), :]
    gates = gates_ref[pl.ds(base, GATE_TOK), :].astype(BF16)
    key1 = (ids >> 7).astype(F32).astype(BF16)
    key2 = (ids & (PEER_NKEYS - 1)).astype(F32).astype(BF16)
    one = jnp.ones((PEER_NKEYS, PEER_SLOTS), BF16)
    zero = jnp.zeros((PEER_NKEYS, PEER_SLOTS), BF16)
    mats = []
    for r in range(GATE_TOK):
        a = jnp.where(sub == key1[r:r + 1, :], one, zero)
        b = jnp.where(sub == key2[r:r + 1, :], jnp.broadcast_to(gates[r:r + 1, :], zero.shape), zero)
        mats.append(_dot(a, b, NT))
    halves = [jnp.swapaxes(jnp.stack(mats[r:r + 8], axis=0), 0, 1) for r in range(0, GATE_TOK, 8)]
    g = jnp.concatenate(halves, axis=1)
    gmat_scr[slot, :, pl.ds(base, GATE_TOK), :] = g.astype(BF16)


def _peer_b_kernel(h2_ref, ids_ref, gates_ref, ids_next_ref, gates_next_ref, ut_ref, v_ref, x_ref, mod_ref,
                   g_ref, b_ref, o_ref, gmat_scr, acc_scr, *, tm):
    i = pl.program_id(0)
    c = pl.program_id(1)
    nch = PEER_NKEYS // PEER_CI
    slot = i % 2

    @pl.when(jnp.logical_and(i == 0, c == 0))
    def _():
        def group(n, carry):
            _build_gates(ids_ref, gates_ref, pl.multiple_of(n * GATE_TOK, GATE_TOK), gmat_scr, 0)
            return carry

        lax.fori_loop(0, tm // GATE_TOK, group, 0)

    act = _dot(h2_ref[...], ut_ref[...])
    ws = []
    for j in range(PEER_CI):
        gj = gmat_scr[slot, c * PEER_CI + j].astype(F32)
        ws.append((gj * jax.nn.gelu(act[:, j * LANES:(j + 1) * LANES])).astype(BF16))
    part = _dot(jnp.concatenate(ws, axis=1), v_ref[...])

    share = tm // nch
    for r in range(share // GATE_TOK):
        _build_gates(ids_next_ref, gates_next_ref, pl.multiple_of(c * share + r * GATE_TOK, GATE_TOK), gmat_scr, 1 - slot)

    @pl.when(c == 0)
    def _():
        acc_scr[...] = part

    @pl.when(c > 0)
    def _():
        acc_scr[...] += part

    @pl.when(c == pl.num_programs(1) - 1)
    def _():
        r = DN_ALPHA * x_ref[...] + _mod_row(mod_ref, 5) * acc_scr[...]
        o_ref[...] = _ln(r) * g_ref[...] + b_ref[...]


def peer_mix(h2, ids, gates, ut, vv, x, mod, ln_g, ln_b, t):
    n = x.shape[0]
    tm = min(256, n)
    ec = PEER_CI * PEER_NKEYS
    row = lambda i, c: (i, 0)
    fixed = lambda i, c: (0, 0)
    last = n // tm - 1
    nxt = lambda i, c: (jnp.minimum(i + 1, last), 0)
    mod, mod_spec = _mod_spec(mod, t, tm, n)
    assert tm % (GATE_TOK * (PEER_NKEYS // PEER_CI)) == 0
    return pl.pallas_call(
        functools.partial(_peer_b_kernel, tm=tm),
        grid=(n // tm, PEER_NKEYS // PEER_CI),
        in_specs=[
            pl.BlockSpec((tm, D_MODEL), row),
            pl.BlockSpec((tm, PEER_SLOTS), row),
            pl.BlockSpec((tm, PEER_SLOTS), row),
            pl.BlockSpec((tm, PEER_SLOTS), nxt),
            pl.BlockSpec((tm, PEER_SLOTS), nxt),
            pl.BlockSpec((None, D_MODEL, ec), lambda i, c: (c, 0, 0)),
            pl.BlockSpec((ec, D_MODEL), lambda i, c: (c, 0)),
            pl.BlockSpec((tm, D_MODEL), row),
            mod_spec,
            pl.BlockSpec((1, D_MODEL), fixed),
            pl.BlockSpec((1, D_MODEL), fixed),
        ],
        out_specs=pl.BlockSpec((tm, D_MODEL), row),
        out_shape=jax.ShapeDtypeStruct((n, D_MODEL), F32),
        scratch_shapes=[pltpu.VMEM((2, PEER_NKEYS, tm, PEER_NKEYS), BF16), pltpu.VMEM((tm, D_MODEL), F32)],
        compiler_params=_cparams("arbitrary", "arbitrary"),
        name="peer_mix",
    )(h2, ids, gates, ids, gates, ut, vv, x, mod, ln_g.reshape(1, D_MODEL), ln_b.reshape(1, D_MODEL))


def _run_group(x3, mod_g, past_k, past_v, gdn_state, conv_state, wts):
    nb, t, _ = x3.shape
    n = nb * t
    cached = past_k is not None
    x = x3.reshape(n, D_MODEL)
    tp = -(-t // CHUNK) * CHUNK
    new_k, new_v, new_s, new_conv, new_sgu_v = [], [], [], [], []
    for l in range(DEPTH):
        mod = mod_g[l]
        pa, q, k, v, pc, pab = in_proj(x, mod, wts["w_main"][l], wts["w_ab"][l], t)
        ya, va = sgu(pa, wts["sgu_ln_g"][l], wts["sgu_ln_b"][l], wts["sgu_w"][l], wts["sgu_b"][l], t)
        lam_init = 0.8 - 0.6 * math.exp(-0.3 * l)
        lam = (jnp.exp(jnp.sum(wts["lam_q1"][l] * wts["lam_k1"][l]))
               - jnp.exp(jnp.sum(wts["lam_q2"][l] * wts["lam_k2"][l])) + lam_init)
        q3, k3, v3 = (a.reshape(nb, t, B_WIDTH) for a in (q, k, v))
        if cached:
            p_len = past_k.shape[2]
            tk = 256
            klen = p_len + t
            padk = -klen % tk
            kk = jnp.concatenate([past_k[l].reshape(nb, p_len, B_QK), k3, jnp.zeros((nb, padk, B_QK), F32)], axis=1)
            vv = jnp.concatenate([past_v[l].reshape(nb, p_len, B_WIDTH), v3, jnp.zeros((nb, padk, B_WIDTH), F32)], axis=1)
            yb = diff_attn(q3, kk, vv, lam, wts["diff_norm_g"][l], 1.0 - lam_init, q_off=p_len, klen=klen, tq=t, tk=tk)
        else:
            yb = diff_attn(q3, k3, v3, lam, wts["diff_norm_g"][l], 1.0 - lam_init, q_off=0, klen=t, tq=min(256, t), tk=min(512, t))
        pc3 = pc.reshape(nb, t, 1024)
        pab3 = pab.reshape(nb, t, LANES)
        if tp != t:
            pc3p = jnp.pad(pc3, ((0, 0), (0, tp - t), (0, 0)))
            pab3p = jnp.pad(pab3, ((0, 0), (0, tp - t), (0, 0)))
        else:
            pc3p, pab3p = pc3, pab3
        cst = conv_state[l] if cached else jnp.zeros((nb, CONV_W - 1, C_QKV), F32)
        s0 = gdn_state[l] if cached else jnp.zeros((nb, C_HEADS, C_DK, C_DV), F32)
        yc, s_fin = gdn(pc3p, pab3p, cst, s0, wts["conv_w"][l], wts["gdn_a_log"][l], wts["gdn_dt_bias"][l],
                        wts["gdn_norm_g"][l], t)
        yc = yc[:, :t].reshape(n, C_WIDTH)
        x = out_proj(x, ya, yb.reshape(n, B_WIDTH), yc, mod, wts["w_out"][l], wts["ln1_g"][l], wts["ln1_b"][l], t)
        h2, ids, gates = peer_retrieve(x, mod, wts["peer_wq"][l], wts["peer_keys"][l], t)
        x = peer_mix(h2, ids.T, gates.T, wts["expert_ut"][l], wts["expert_v"][l], x, mod,
                     wts["ln2_g"][l], wts["ln2_b"][l], t)
        new_k.append(k3.reshape(nb, t, B_HEADS, 2 * B_DK))
        new_v.append(v3.reshape(nb, t, B_HEADS, B_DV))
        new_s.append(s_fin)
        xp_tail = jnp.concatenate([cst, pc3[:, :, :C_QKV]], axis=1)[:, -(CONV_W - 1):] if t < CONV_W - 1 \
            else pc3[:, t - (CONV_W - 1):, :C_QKV]
        new_conv.append(xp_tail)
        new_sgu_v.append(va.reshape(nb, t, A_WIDTH))
    return (x.reshape(nb, t, D_MODEL), jnp.stack(new_k), jnp.stack(new_v), jnp.stack(new_s), jnp.stack(new_conv),
            jnp.stack(new_sgu_v))


def kernel(x_prompt, x_sample, cache_k, cache_v, state_gdn, state_conv, c_prompt, c_sample,
           w_ada, b_ada, w_in, sgu_ln_g, sgu_ln_b, sgu_w, sgu_b,
           lam_q1, lam_k1, lam_q2, lam_k2, diff_norm_g,
           conv_w, gdn_a_log, gdn_dt_bias, gdn_norm_g,
           w_out, ln1_g, ln1_b, peer_wq, peer_keys, expert_u, expert_v, ln2_g, ln2_b):
    nbp = x_prompt.shape[0]
    nbs = x_sample.shape[0]
    wts = dict(
        w_main=w_in[:, :, :P_MAIN].astype(BF16),
        w_ab=jnp.pad(w_in[:, :, P_MAIN:], ((0, 0), (0, 0), (0, LANES - (P_IN - P_MAIN)))).astype(BF16),
        sgu_ln_g=sgu_ln_g, sgu_ln_b=sgu_ln_b, sgu_w=sgu_w, sgu_b=sgu_b,
        lam_q1=lam_q1, lam_k1=lam_k1, lam_q2=lam_q2, lam_k2=lam_k2, diff_norm_g=diff_norm_g,
        conv_w=conv_w, gdn_a_log=gdn_a_log, gdn_dt_bias=gdn_dt_bias, gdn_norm_g=gdn_norm_g,
        w_out=w_out.astype(BF16), ln1_g=ln1_g, ln1_b=ln1_b,
        peer_wq=peer_wq.astype(BF16),
        peer_keys=peer_keys.reshape(DEPTH, 2 * PEER_HEADS, PEER_NKEYS, PEER_DQ // 2).astype(BF16),
        expert_ut=jnp.swapaxes(expert_u.astype(BF16).reshape(DEPTH, PEER_NKEYS // PEER_CI, PEER_CI * PEER_NKEYS, D_MODEL),
                               2, 3),
        expert_v=expert_v.astype(BF16),
        ln2_g=ln2_g, ln2_b=ln2_b,
    )
    mod_all = ada_mod(jnp.concatenate([c_prompt, c_sample], axis=0), w_ada, b_ada)
    mod_all = mod_all.reshape(DEPTH, nbp + nbs, 6, D_MODEL)
    y_p, k_p, v_p, s_p, conv_p, _ = _run_group(x_prompt, mod_all[:, :nbp], None, None, None, None, wts)
    y_s, k_s, v_s, s_s, conv_s, sgu_s = _run_group(x_sample, mod_all[:, nbp:], cache_k, cache_v, state_gdn, state_conv, wts)
    return (y_p, y_s, k_p, v_p, s_p, conv_p, k_s, v_s, s_s, conv_s, sgu_s)
```

```python
import functools
import math

import jax
import jax.numpy as jnp
from jax import lax
from jax.experimental import pallas as pl
from jax.experimental.pallas import tpu as pltpu

F32 = jnp.float32
BF16 = jnp.bfloat16

D_MODEL = 1024
DEPTH = 4
CHUNK = 64
A_GROUPS = 4
A_GDIM = 64
A_WIDTH = 256
SGU_LEN = 128
B_HEADS = 4
B_DK = 64
B_DV = 128
B_WIDTH = 512
B_QK = 512
C_HEADS = 4
C_DK = 64
C_DV = 64
C_WIDTH = 256
CONV_W = 4
C_QKV = 768
P_MAIN = 3072
P_IN = 3080
PEER_HEADS = 8
PEER_NKEYS = 128
PEER_DQ = 256
PEER_TOPK = 16
PEER_SLOTS = PEER_HEADS * PEER_TOPK
DN_ALPHA = (2 * DEPTH) ** 0.25
LN_EPS = 1e-5
NEG = -1e30

LANES = 128
VMEM_LIMIT = 56 * 1024 * 1024

NN = (((1,), (0,)), ((), ()))
NT = (((1,), (1,)), ((), ()))


def _cparams(*sem):
    return pltpu.CompilerParams(dimension_semantics=sem, vmem_limit_bytes=VMEM_LIMIT)


def _dot(a, b, dims=NN):
    return lax.dot_general(a, b, dims, preferred_element_type=F32)


def _split(x, n):
    parts = []
    r = x
    for _ in range(n - 1):
        p = r.astype(BF16)
        parts.append(p)
        r = r - p.astype(F32)
    parts.append(r.astype(BF16))
    return parts


def _dot3(a, b, dims=NN):
    ah, al = _split(a, 2)
    bh, bl = _split(b, 2)
    return _dot(ah, bh, dims) + (_dot(ah, bl, dims) + _dot(al, bh, dims))


def _dot_exact_rhs(x, m_bf16, n=3, dims=NN):
    parts = _split(x, n)
    acc = _dot(parts[-1], m_bf16, dims)
    for p in parts[-2::-1]:
        acc = acc + _dot(p, m_bf16, dims)
    return acc


def _dot_exact_lhs(m_bf16, x, n=3, dims=NN):
    parts = _split(x, n)
    acc = _dot(m_bf16, parts[-1], dims)
    for p in parts[-2::-1]:
        acc = acc + _dot(m_bf16, p, dims)
    return acc


def _ln(x):
    mu = jnp.mean(x, axis=-1, keepdims=True)
    d = x - mu
    var = jnp.mean(d * d, axis=-1, keepdims=True)
    return d * lax.rsqrt(var + LN_EPS)


def _iota(shape, dim):
    return lax.broadcasted_iota(jnp.int32, shape, dim)


def _ada_kernel(c_ref, w_ref, b_ref, o_ref):
    c = c_ref[...]
    s = c * jax.nn.sigmoid(c)
    o_ref[0] = _dot3(s, w_ref[0]) + b_ref[0]


def ada_mod(c_all, w_ada, b_ada):
    nbt = c_all.shape[0]
    tn = 1536
    return pl.pallas_call(
        _ada_kernel,
        grid=(DEPTH, 6 * D_MODEL // tn),
        in_specs=[
            pl.BlockSpec((nbt, D_MODEL), lambda l, j: (0, 0)),
            pl.BlockSpec((1, D_MODEL, tn), lambda l, j: (l, 0, j)),
            pl.BlockSpec((1, 1, tn), lambda l, j: (l, 0, j)),
        ],
        out_specs=pl.BlockSpec((1, nbt, tn), lambda l, j: (l, 0, j)),
        out_shape=jax.ShapeDtypeStruct((DEPTH, nbt, 6 * D_MODEL), F32),
        compiler_params=_cparams("parallel", "parallel"),
        name="ada_mod",
    )(c_all, w_ada, b_ada.reshape(DEPTH, 1, 6 * D_MODEL))


def _in_kernel(x_ref, mod_ref, w_ref, wab_ref, pa_ref, q_ref, k_ref, v_ref, pc_ref, pab_ref):
    h = _ln(x_ref[...]) * (1.0 + mod_ref[0, 1:2, :]) + mod_ref[0, 0:1, :]
    hb = h.astype(BF16)
    pa_ref[...] = _dot(hb, w_ref[:, 0:512])
    q_ref[...] = _dot(hb, w_ref[:, 512:1024])
    k_ref[...] = _dot(hb, w_ref[:, 1024:1536])
    v_ref[...] = _dot(hb, w_ref[:, 1536:2048])
    pc_ref[...] = _dot(hb, w_ref[:, 2048:3072])
    pab_ref[...] = _dot(hb, wab_ref[...])


def in_proj(x, mod, w_main, w_ab, t):
    n = x.shape[0]
    tm = min(512, t)
    per = t // tm
    row = lambda i: (i, 0)
    fixed = lambda i: (0, 0)
    sds = lambda w: jax.ShapeDtypeStruct((n, w), F32)
    return pl.pallas_call(
        _in_kernel,
        grid=(n // tm,),
        in_specs=[
            pl.BlockSpec((tm, D_MODEL), row),
            pl.BlockSpec((1, 6, D_MODEL), lambda i: (i // per, 0, 0)),
            pl.BlockSpec((D_MODEL, P_MAIN), fixed),
            pl.BlockSpec((D_MODEL, LANES), fixed),
        ],
        out_specs=[pl.BlockSpec((tm, 512), row)] * 4 + [pl.BlockSpec((tm, 1024), row), pl.BlockSpec((tm, LANES), row)],
        out_shape=[sds(512)] * 4 + [sds(1024), sds(LANES)],
        compiler_params=_cparams("parallel"),
        name="in_proj",
    )(x, mod, w_main, w_ab)


def _group_ones(width, gdim):
    r = _iota((width, width), 0) // gdim
    c = _iota((width, width), 1) // gdim
    return jnp.where(r == c, 1.0, 0.0).astype(BF16)


def _sgu_kernel(pa_ref, g_ref, b_ref, w_ref, bs_ref, ya_ref, va_ref, *, blk, nblk):
    pa = pa_ref[...]
    u = jax.nn.gelu(pa[:, :A_WIDTH])
    v = jax.nn.gelu(pa[:, A_WIDTH:])
    ones = _group_ones(A_WIDTH, A_GDIM)
    mean = _dot_exact_rhs(v, ones) * (1.0 / A_GDIM)
    d = v - mean
    var = _dot_exact_rhs(d * d, ones) * (1.0 / A_GDIM)
    vn = d * lax.rsqrt(var + LN_EPS) * g_ref[...] + b_ref[...]
    va_ref[...] = vn
    ri = _iota((blk, blk), 0) // CHUNK
    ci = _iota((blk, blk), 1) // CHUNK
    lane_group = _iota((blk, A_WIDTH), 1) // A_GDIM
    wm = [jnp.where(ci <= ri, w_ref[g], 0.0).astype(BF16) for g in range(A_GROUPS)]
    for r in range(nblk):
        vb = vn[r * blk:(r + 1) * blk, :].astype(BF16)
        s = jnp.zeros((blk, A_WIDTH), F32)
        for g in range(A_GROUPS):
            s = jnp.where(lane_group == g, _dot(wm[g], vb), s)
        ya_ref[r * blk:(r + 1) * blk, :] = u[r * blk:(r + 1) * blk, :] * (s + bs_ref[...])


def sgu(pa, ln_g, ln_b, w_s, b_s, t):
    n = pa.shape[0]
    blk = min(SGU_LEN, t)
    assert t % blk == 0
    tm = min(512, n)
    w = w_s[:, :blk, :blk]
    bs = jnp.repeat(b_s.T[:blk], A_GDIM, axis=1)
    row = lambda i: (i, 0)
    return pl.pallas_call(
        functools.partial(_sgu_kernel, blk=blk, nblk=tm // blk),
        grid=(n // tm,),
        in_specs=[
            pl.BlockSpec((tm, 2 * A_WIDTH), row),
            pl.BlockSpec((1, A_WIDTH), lambda i: (0, 0)),
            pl.BlockSpec((1, A_WIDTH), lambda i: (0, 0)),
            pl.BlockSpec((A_GROUPS, blk, blk), lambda i: (0, 0, 0)),
            pl.BlockSpec((blk, A_WIDTH), lambda i: (0, 0)),
        ],
        out_specs=[pl.BlockSpec((tm, A_WIDTH), row)] * 2,
        out_shape=[jax.ShapeDtypeStruct((n, A_WIDTH), F32)] * 2,
        compiler_params=_cparams("parallel"),
        name="sgu",
    )(pa, ln_g.reshape(1, A_WIDTH), ln_b.reshape(1, A_WIDTH), w, bs)


ATTN_HEADS_PER_STEP = 2


def _attn_kernel(lam_ref, slope_ref, q_ref, k_ref, v_ref, g_ref, o_ref, *, tq, tk, q_off, klen, out_scale, nh):
    hg = pl.program_id(1)
    qi = pl.program_id(2)
    rows = 2 * tq
    log2e = 1.4426950408889634
    lane = _iota((tq, B_DV), 1)
    q_lo = q_off + qi * tq
    qpos = q_lo + (_iota((rows, tk), 0) & (tq - 1))
    col = _iota((rows, tk), 1)
    colf = _iota((1, tk), 1).astype(F32)
    limit = jnp.minimum(klen, ((q_lo + tq - 1) // CHUNK + 1) * CHUNK)
    nkb = (limit + tk - 1) // tk
    nfull = jnp.minimum(q_lo, klen) // tk
    heads = range(nh)
    hl = [slice(e * B_DV, (e + 1) * B_DV) for e in heads]
    slope2 = [slope_ref[hg * nh + e] * log2e for e in heads]
    qq = []
    for e in heads:
        q = q_ref[0, :, hl[e]] * (B_DK ** -0.5 * log2e)
        qq.append(jnp.concatenate([jnp.where(lane < B_DK, q, 0.0), jnp.where(lane >= B_DK, q, 0.0)],
                                  axis=0).astype(BF16))

    def scores(e, kb):
        start = pl.multiple_of(kb * tk, tk)
        return _dot(qq[e], k_ref[0, pl.ds(start, tk), hl[e]].astype(BF16), NT)

    def update(e, kb, s, m, l, acc):
        start = pl.multiple_of(kb * tk, tk)
        vblk = v_ref[0, pl.ds(start, tk), hl[e]].astype(BF16)
        m_new = jnp.maximum(m, jnp.max(s, axis=1, keepdims=True))
        alpha = jnp.exp2(m - m_new)
        p = jnp.exp2(s - m_new)
        l = alpha * l + jnp.sum(p, axis=1, keepdims=True)
        acc = alpha * acc + _dot(p.astype(BF16), vblk)
        return m_new, l, acc

    def masked(e, kb, s):
        kpos = kb * tk + col
        s = s + slope2[e] * (qpos - jnp.abs(qpos - kpos)).astype(F32)
        kchunk = jnp.where(kpos < klen, kpos // CHUNK, jnp.int32(1 << 30))
        return jnp.where(kchunk <= (qpos // CHUNK), s, NEG)

    def full_body(kb, carry):
        ss = [scores(e, kb) + slope2[e] * ((kb * tk).astype(F32) + colf) for e in heads]
        return tuple(update(e, kb, ss[e], *carry[e]) for e in heads)

    def tail_body(kb, carry):
        ss = [masked(e, kb, scores(e, kb)) for e in heads]
        return tuple(update(e, kb, ss[e], *carry[e]) for e in heads)

    init = tuple((jnp.full((rows, 1), NEG, F32), jnp.zeros((rows, 1), F32), jnp.zeros((rows, B_DV), F32))
                 for e in heads)
    carry = lax.fori_loop(0, nfull, full_body, init)
    carry = lax.fori_loop(nfull, nkb, tail_body, carry)
    for e in heads:
        m, l, acc = carry[e]
        o = acc / l
        ob = o[:tq] - lam_ref[0] * o[tq:]
        ms = jnp.mean(ob * ob, axis=-1, keepdims=True)
        o_ref[0, :, hl[e]] = ob * lax.rsqrt(ms + LN_EPS) * g_ref[...] * out_scale


def diff_attn(q, k, v, lam, norm_g, out_scale, *, q_off, klen, tq, tk):
    nb, t, _ = q.shape
    tkp = k.shape[1]
    assert klen >= q_off + t and tkp % tk == 0 and tkp >= klen and t % tq == 0
    slopes = jnp.exp2(-(8.0 / B_HEADS) * jnp.arange(1, B_HEADS + 1, dtype=F32))
    smem = pl.BlockSpec(memory_space=pltpu.SMEM)
    nh = ATTN_HEADS_PER_STEP
    return pl.pallas_call(
        functools.partial(_attn_kernel, tq=tq, tk=tk, q_off=q_off, klen=klen, out_scale=out_scale, nh=nh),
        grid=(nb, B_HEADS // nh, t // tq),
        in_specs=[
            smem, smem,
            pl.BlockSpec((1, tq, nh * B_DV), lambda b, h, i: (b, i, h)),
            pl.BlockSpec((1, tkp, nh * B_DV), lambda b, h, i: (b, 0, h)),
            pl.BlockSpec((1, tkp, nh * B_DV), lambda b, h, i: (b, 0, h)),
            pl.BlockSpec((1, B_DV), lambda b, h, i: (0, 0)),
        ],
        out_specs=pl.BlockSpec((1, tq, nh * B_DV), lambda b, h, i: (b, i, h)),
        out_shape=jax.ShapeDtypeStruct((nb, t, B_WIDTH), F32),
        compiler_params=_cparams("parallel", "parallel", "arbitrary"),
        name="diff_attn",
    )(lam.reshape(1), slopes, q, k, v, norm_g.reshape(1, B_DV))


GDN_ROWS = 8


def _gdn_kernel(pc_ref, pab_ref, cst_ref, cw_ref, alog_ref, dtb_ref, ng_ref, s0_ref,
                yc_ref, sfin_ref, prev_scr, *, t_valid, bb):
    c = pl.program_id(1)

    @pl.when(c == 0)
    def _():
        prev_scr[...] = cst_ref[...]
        sfin_ref[...] = s0_ref[...]

    rows = [_gdn_chunk(bi, c, pc_ref, pab_ref, cw_ref, alog_ref, dtb_ref, ng_ref, yc_ref, sfin_ref, prev_scr, t_valid)
            for bi in range(bb)]
    while rows:
        rows = [r for r in rows if next(r, True) is None]


def _gdn_chunk(bi, c, pc_ref, pab_ref, cw_ref, alog_ref, dtb_ref, ng_ref, yc_ref, sfin_ref, prev_scr, t_valid):
    W = C_WIDTH
    qkv = pc_ref[bi, :, 0:C_QKV]
    z = pc_ref[bi, :, C_QKV:C_QKV + W]
    ext = jnp.concatenate([prev_scr[bi], qkv], axis=0)
    prev_scr[bi] = qkv[CHUNK - 8:, :]
    conv = ext[5:5 + CHUNK] * cw_ref[0:1, :]
    for j in range(1, CONV_W):
        conv = conv + ext[5 + j:5 + j + CHUNK] * cw_ref[j:j + 1, :]
    conv = conv * jax.nn.sigmoid(conv)
    q, k, v = conv[:, 0:W], conv[:, W:2 * W], conv[:, 2 * W:3 * W]

    head_ones = _group_ones(W, C_DK)
    qss = _dot_exact_rhs(q * q, head_ones)
    kss = _dot_exact_rhs(k * k, head_ones)
    pab = pab_ref[bi]
    g_cols = -jnp.exp(alog_ref[...]) * jax.nn.softplus(pab + dtb_ref[...])
    b_cols = jax.nn.sigmoid(pab)
    er = _iota((LANES, W), 0)
    eh = _iota((LANES, W), 1) // C_DK
    g = _dot_exact_rhs(g_cols, jnp.where(er == eh, 1.0, 0.0).astype(BF16))
    beta = _dot_exact_rhs(b_cols, jnp.where(er == eh + C_HEADS, 1.0, 0.0).astype(BF16))
    yield
    q = q * lax.rsqrt(qss + 1e-6) * (C_DK ** -0.5)
    k = k * lax.rsqrt(kss + 1e-6)
    if t_valid % CHUNK:
        valid = (c * CHUNK + _iota((CHUNK, W), 0)) < t_valid
        q, k, v = (jnp.where(valid, a, 0.0) for a in (q, k, v))
        g = jnp.where(valid, g, 0.0)
        beta = jnp.where(valid, beta, 0.0)

    tri = jnp.where(_iota((CHUNK, CHUNK), 0) >= _iota((CHUNK, CHUNK), 1), 1.0, 0.0).astype(BF16)
    gc = _dot_exact_lhs(tri, g)
    yield
    eg = jnp.exp(gc)
    glast = gc[CHUNK - 1:CHUNK, :]
    kdec = k * jnp.exp(glast - gc)
    qg = q * eg
    kb = k * beta
    kbe = kb * eg
    vb = v * beta

    R = C_HEADS * CHUNK
    same = (_iota((R, W), 0) // CHUNK) == (_iota((R, W), 1) // C_DK)
    ii = _iota((CHUNK, W), 0)
    jj = _iota((CHUNK, W), 1) % CHUNK

    same16 = jnp.where(same, 1.0, 0.0).astype(BF16)

    def bd(a):
        return jnp.where(same, jnp.concatenate([a] * C_HEADS, axis=0), 0.0)

    def bd16(a16):
        return jnp.concatenate([a16] * C_HEADS, axis=0) * same16

    def split_bd(a):
        hi = a.astype(BF16)
        lo = (a - hi.astype(F32)).astype(BF16)
        return hi, lo, bd16(hi), bd16(lo)

    def mm3(a_hi, a_lo, b_hi, b_lo):
        return _dot(a_hi, b_hi) + (_dot(a_hi, b_lo) + _dot(a_lo, b_hi))

    kx = bd16(k.astype(BF16))
    kk = _dot(kb.astype(BF16), kx, NT)
    qk = _dot(q.astype(BF16), kx, NT)
    pick = jnp.where(jj == 0, 1.0, 0.0).astype(BF16)
    gparts = _split(gc, 3)
    g_col = _dot(pick, bd16(gparts[2]), NT)
    g_col = g_col + _dot(pick, bd16(gparts[1]), NT)
    g_col = g_col + _dot(pick, bd16(gparts[0]), NT)
    yield
    lower = ii >= jj
    gamma = jnp.where(lower, jnp.exp(jnp.where(lower, gc - g_col, 0.0)), 0.0)
    nm = jnp.where(ii > jj, -(kk * gamma), 0.0)
    aqk = qk * gamma

    tinv = jnp.where(ii == jj, 1.0, 0.0) + nm
    p_hi, p_lo, pb_hi, pb_lo = split_bd(nm)
    for step in range(5):
        pw = mm3(p_hi, p_lo, pb_hi, pb_lo)
        yield
        p_hi, p_lo, pb_hi, pb_lo = split_bd(pw)
        t_hi, t_lo = _split(tinv, 2)
        tinv = tinv + mm3(t_hi, t_lo, pb_hi, pb_lo)
    _, _, vb_hi, vb_lo = split_bd(vb)
    _, _, ke_hi, ke_lo = split_bd(kbe)
    yield
    t_hi, t_lo = _split(tinv, 2)
    u = mm3(t_hi, t_lo, vb_hi, vb_lo)
    w = mm3(t_hi, t_lo, ke_hi, ke_lo)
    s = sfin_ref[bi]
    sb = s.astype(BF16)
    oq = _dot(qg.astype(BF16), sb)
    kdt = kdec.T.astype(BF16)
    yield
    v_new = u - _dot(w.astype(BF16), sb)
    yield
    vnb = v_new.astype(BF16)
    o = oq + _dot(aqk.astype(BF16), bd16(vnb))
    sfin_ref[bi] = s * jnp.exp(glast) + jnp.where(same, _dot(kdt, vnb), 0.0)
    yield
    ms = _dot_exact_rhs(o * o, head_ones) * (1.0 / C_DV)
    yield
    yc_ref[bi] = o * lax.rsqrt(ms + LN_EPS) * ng_ref[...] * (z * jax.nn.sigmoid(z))


def gdn(pc, pab, conv_state, s0, conv_w, a_log, dt_bias, norm_g, t_valid):
    nb, tp, _ = pc.shape
    cst = jnp.pad(conv_state, ((0, 0), (8 - (CONV_W - 1), 0), (0, 0)))
    cw = jnp.pad(conv_w, ((0, 8 - CONV_W), (0, 0)))
    pad_row = lambda a: jnp.pad(a.reshape(1, C_HEADS), ((0, 0), (0, LANES - C_HEADS)))
    eye = jnp.eye(C_HEADS, dtype=F32)
    s0_bd = jnp.einsum('bhde,hg->bhdge', s0, eye).reshape(nb, C_HEADS * C_DK, C_HEADS * C_DV)
    full = lambda b, c: (0, 0)
    bb = math.gcd(nb, GDN_ROWS)
    yc, sfin = pl.pallas_call(
        functools.partial(_gdn_kernel, t_valid=t_valid, bb=bb),
        grid=(nb // bb, tp // CHUNK),
        in_specs=[
            pl.BlockSpec((bb, CHUNK, 1024), lambda b, c: (b, c, 0)),
            pl.BlockSpec((bb, CHUNK, LANES), lambda b, c: (b, c, 0)),
            pl.BlockSpec((bb, 8, C_QKV), lambda b, c: (b, 0, 0)),
            pl.BlockSpec((8, C_QKV), full),
            pl.BlockSpec((1, LANES), full),
            pl.BlockSpec((1, LANES), full),
            pl.BlockSpec((1, C_WIDTH), full),
            pl.BlockSpec((bb, C_WIDTH, C_WIDTH), lambda b, c: (b, 0, 0)),
        ],
        out_specs=[
            pl.BlockSpec((bb, CHUNK, C_WIDTH), lambda b, c: (b, c, 0)),
            pl.BlockSpec((bb, C_WIDTH, C_WIDTH), lambda b, c: (b, 0, 0)),
        ],
        out_shape=[
            jax.ShapeDtypeStruct((nb, tp, C_WIDTH), F32),
            jax.ShapeDtypeStruct((nb, C_WIDTH, C_WIDTH), F32),
        ],
        scratch_shapes=[pltpu.VMEM((bb, 8, C_QKV), F32)],
        compiler_params=_cparams("parallel", "arbitrary"),
        name="gdn",
    )(pc, pab, cst, cw, pad_row(a_log), pad_row(dt_bias), jnp.tile(norm_g.reshape(1, C_DV), (1, C_HEADS)), s0_bd)
    s5 = sfin.reshape(nb, C_HEADS, C_DK, C_HEADS, C_DV)
    s_fin = jnp.stack([s5[:, h, :, h, :] for h in range(C_HEADS)], axis=1)
    return yc, s_fin


def _out_kernel(x_ref, ya_ref, yb_ref, yc_ref, mod_ref, w_ref, g_ref, b_ref, o_ref):
    y = _dot(ya_ref[...].astype(BF16), w_ref[0:256, :])
    y = y + _dot(yb_ref[...].astype(BF16), w_ref[256:768, :])
    y = y + _dot(yc_ref[...].astype(BF16), w_ref[768:1024, :])
    r = DN_ALPHA * x_ref[...] + mod_ref[0, 2:3, :] * y
    o_ref[...] = _ln(r) * g_ref[...] + b_ref[...]


def out_proj(x, ya, yb, yc, mod, w_out, ln_g, ln_b, t):
    n = x.shape[0]
    tm = min(512, t)
    per = t // tm
    row = lambda i: (i, 0)
    fixed = lambda i: (0, 0)
    return pl.pallas_call(
        _out_kernel,
        grid=(n // tm,),
        in_specs=[
            pl.BlockSpec((tm, D_MODEL), row),
            pl.BlockSpec((tm, A_WIDTH), row),
            pl.BlockSpec((tm, B_WIDTH), row),
            pl.BlockSpec((tm, C_WIDTH), row),
            pl.BlockSpec((1, 6, D_MODEL), lambda i: (i // per, 0, 0)),
            pl.BlockSpec((D_MODEL, D_MODEL), fixed),
            pl.BlockSpec((1, D_MODEL), fixed),
            pl.BlockSpec((1, D_MODEL), fixed),
        ],
        out_specs=pl.BlockSpec((tm, D_MODEL), row),
        out_shape=jax.ShapeDtypeStruct((n, D_MODEL), F32),
        compiler_params=_cparams("parallel"),
        name="out_proj",
    )(x, ya, yb, yc, mod, w_out, ln_g.reshape(1, D_MODEL), ln_b.reshape(1, D_MODEL))


def _round_robin(gens):
    gens = list(gens)
    while gens:
        gens = [g for g in gens if next(g, True) is None]


def _extract_top(vals, order, out_s_ref, out_p_ref, base, lanes):
    big = 1e9
    for j in range(PEER_TOPK):
        m = jnp.max(vals, axis=0, keepdims=True)
        tied = jnp.where(vals == m, order, big)
        pos = jnp.min(tied, axis=0, keepdims=True)
        out_s_ref[base + j:base + j + 1, lanes] = m
        out_p_ref[base + j:base + j + 1, lanes] = pos
        vals = jnp.where(tied == pos, -jnp.inf, vals)
        yield


def _extract_top_keys(scores, out_s_ref, out_i_ref, base, lanes):
    half = PEER_NKEYS // 2
    a, b = scores[0:half], scores[half:PEER_NKEYS]
    ia = _iota((half, LANES), 0).astype(F32)
    ib = ia + float(half)
    swap = b > a
    top, rest = jnp.where(swap, b, a), jnp.where(swap, a, b)
    itop, irest = jnp.where(swap, ib, ia), jnp.where(swap, ia, ib)
    big = 1e9
    for j in range(PEER_TOPK):
        m = jnp.max(top, axis=0, keepdims=True)
        tied = jnp.where(top == m, itop, big)
        pos = jnp.min(tied, axis=0, keepdims=True)
        out_s_ref[base + j:base + j + 1, lanes] = m
        out_i_ref[base + j:base + j + 1, lanes] = pos
        hit = tied == pos
        top = jnp.where(hit, rest, top)
        itop = jnp.where(hit, irest, itop)
        rest = jnp.where(hit, -jnp.inf, rest)
        yield


def _mod_row(mod_ref, i):
    return mod_ref[0, i:i + 1, :] if len(mod_ref.shape) == 3 else mod_ref[0, i]


def _mod_spec(mod, t, tm, n):
    if tm > t:
        mod = jnp.repeat(mod, t, axis=0).reshape(n // tm, tm, 6, D_MODEL).transpose(0, 2, 1, 3)
        return mod, pl.BlockSpec((1, 6, tm, D_MODEL), lambda i, *_: (i, 0, 0, 0))
    per = t // tm
    return mod, pl.BlockSpec((1, 6, D_MODEL), lambda i, *_: (i // per, 0, 0))


def _peer_a_kernel(x_ref, mod_ref, wq_ref, keys_ref, h2_ref, ids_ref, gates_ref, ts_scr, ti_scr, bs_scr, id_scr):
    tm = x_ref.shape[0]
    h2 = _ln(x_ref[...]) * (1.0 + _mod_row(mod_ref, 4)) + _mod_row(mod_ref, 3)
    hb = h2.astype(BF16)
    h2_ref[...] = hb
    groups = [slice(g * LANES, (g + 1) * LANES) for g in range(tm // LANES)]
    for hp in range(2 * PEER_HEADS):
        qhp = _dot(hb, wq_ref[:, hp * LANES:(hp + 1) * LANES]).astype(BF16)
        st = _dot(keys_ref[hp], qhp, NT)
        _round_robin(_extract_top_keys(st[:, lanes], ts_scr, ti_scr, hp * PEER_TOPK, lanes) for lanes in groups)
    r8i = _iota((8, LANES), 0)
    r8 = r8i.astype(F32)
    for h in range(PEER_HEADS):
        b0 = 2 * h * PEER_TOPK
        b1 = b0 + PEER_TOPK
        rounds = []
        for lanes in groups:
            cand, order = [], []

            def add(sa, sb, rows_a, lo, hi):
                va = ts_scr[b0 + sa[0]:b0 + sa[0] + sa[1], lanes]
                vb = ts_scr[b1 + sb[0]:b1 + sb[0] + sb[1], lanes]
                s = va + vb
                if (lo, hi) != (0, 8):
                    s = jnp.where(r8i >= lo, jnp.where(r8i < hi, s, -jnp.inf), -jnp.inf)
                cand.append(s)
                order.append((sa[0] + r8) * float(PEER_TOPK) + sb[0] if rows_a else sa[0] * PEER_TOPK + sb[0] + r8)

            add((0, 1), (0, 8), False, 0, 8)
            add((0, 1), (8, 8), False, 0, 8)
            add((1, 1), (0, 8), False, 0, 8)
            add((8, 8), (0, 1), True, 0, 8)
            for b in range(5):
                add((0, 8), (b, 1), True, 2, PEER_TOPK // (b + 1))
            rounds.append(_extract_top(jnp.concatenate(cand, axis=0), jnp.concatenate(order, axis=0),
                                       bs_scr, id_scr, h * PEER_TOPK, lanes))
        _round_robin(rounds)
        rows = slice(h * PEER_TOPK, (h + 1) * PEER_TOPK)
        pick = id_scr[rows, :].astype(jnp.int32)
        pa, pb = pick >> 4, pick & (PEER_TOPK - 1)
        key1 = jnp.zeros((PEER_TOPK, tm), F32)
        key2 = jnp.zeros((PEER_TOPK, tm), F32)
        for r in range(PEER_TOPK):
            key1 = jnp.where(pa == r, ti_scr[b0 + r:b0 + r + 1, :], key1)
            key2 = jnp.where(pb == r, ti_scr[b1 + r:b1 + r + 1, :], key2)
        ids_ref[rows, :] = (key1 * float(PEER_NKEYS) + key2).astype(jnp.int32)
        bs = bs_scr[rows, :]
        e = jnp.exp(bs - bs[0:1, :])
        gates_ref[rows, :] = e / jnp.sum(e, axis=0, keepdims=True)


def peer_retrieve(x, mod, wq, keys, t):
    n = x.shape[0]
    tm = min(256, n)
    row = lambda i: (i, 0)
    col = lambda i: (0, i)
    mod, mod_spec = _mod_spec(mod, t, tm, n)
    return pl.pallas_call(
        _peer_a_kernel,
        grid=(n // tm,),
        in_specs=[
            pl.BlockSpec((tm, D_MODEL), row),
            mod_spec,
            pl.BlockSpec((D_MODEL, PEER_HEADS * PEER_DQ), lambda i: (0, 0)),
            pl.BlockSpec((2 * PEER_HEADS, PEER_NKEYS, PEER_DQ // 2), lambda i: (0, 0, 0)),
        ],
        out_specs=[pl.BlockSpec((tm, D_MODEL), row), pl.BlockSpec((PEER_SLOTS, tm), col), pl.BlockSpec((PEER_SLOTS, tm), col)],
        out_shape=[
            jax.ShapeDtypeStruct((n, D_MODEL), BF16),
            jax.ShapeDtypeStruct((PEER_SLOTS, n), jnp.int32),
            jax.ShapeDtypeStruct((PEER_SLOTS, n), F32),
        ],
        scratch_shapes=[
            pltpu.VMEM((2 * PEER_HEADS * PEER_TOPK, tm), F32),
            pltpu.VMEM((2 * PEER_HEADS * PEER_TOPK, tm), F32),
            pltpu.VMEM((PEER_SLOTS, tm), F32),
            pltpu.VMEM((PEER_SLOTS, tm), F32),
        ],
        compiler_params=_cparams("parallel"),
        name="peer_retrieve",
    )(x, mod, wq, keys)


PEER_CI = 16


GATE_TOK = 16


def _build_gates(ids_ref, gates_ref, base, gmat_scr, slot):
    sub = _iota((PEER_NKEYS, PEER_SLOTS), 0).astype(F32).astype(BF16)
    ids = ids_ref[pl.ds(base, GATE_TOK), :]
    gates = gates_ref[pl.ds(base, GATE_TOK), :].astype(BF16)
    key1 = (ids >> 7).astype(F32).astype(BF16)
    key2 = (ids & (PEER_NKEYS - 1)).astype(F32).astype(BF16)
    one = jnp.ones((PEER_NKEYS, PEER_SLOTS), BF16)
    zero = jnp.zeros((PEER_NKEYS, PEER_SLOTS), BF16)
    mats = []
    for r in range(GATE_TOK):
        a = jnp.where(sub == key1[r:r + 1, :], one, zero)
        b = jnp.where(sub == key2[r:r + 1, :], jnp.broadcast_to(gates[r:r + 1, :], zero.shape), zero)
        mats.append(_dot(a, b, NT))
    halves = [jnp.swapaxes(jnp.stack(mats[r:r + 8], axis=0), 0, 1) for r in range(0, GATE_TOK, 8)]
    g = jnp.concatenate(halves, axis=1)
    gmat_scr[slot, :, pl.ds(base, GATE_TOK), :] = g.astype(BF16)


def _peer_b_kernel(h2_ref, ids_ref, gates_ref, ids_next_ref, gates_next_ref, ut_ref, v_ref, x_ref, mod_ref,
                   g_ref, b_ref, o_ref, gmat_scr, acc_scr, *, tm):
    i = pl.program_id(0)
    c = pl.program_id(1)
    nch = PEER_NKEYS // PEER_CI
    slot = i % 2

    @pl.when(jnp.logical_and(i == 0, c == 0))
    def _():
        def group(n, carry):
            _build_gates(ids_ref, gates_ref, pl.multiple_of(n * GATE_TOK, GATE_TOK), gmat_scr, 0)
            return carry

        lax.fori_loop(0, tm // GATE_TOK, group, 0)

    act = _dot(h2_ref[...], ut_ref[...])
    ws = []
    for j in range(PEER_CI):
        gj = gmat_scr[slot, c * PEER_CI + j].astype(F32)
        ws.append((gj * jax.nn.gelu(act[:, j * LANES:(j + 1) * LANES])).astype(BF16))
    part = _dot(jnp.concatenate(ws, axis=1), v_ref[...])

    share = tm // nch
    for r in range(share // GATE_TOK):
        _build_gates(ids_next_ref, gates_next_ref, pl.multiple_of(c * share + r * GATE_TOK, GATE_TOK), gmat_scr, 1 - slot)

    @pl.when(c == 0)
    def _():
        acc_scr[...] = part

    @pl.when(c > 0)
    def _():
        acc_scr[...] += part

    @pl.when(c == pl.num_programs(1) - 1)
    def _():
        r = DN_ALPHA * x_ref[...] + _mod_row(mod_ref, 5) * acc_scr[...]
        o_ref[...] = _ln(r) * g_ref[...] + b_ref[...]


def peer_mix(h2, ids, gates, ut, vv, x, mod, ln_g, ln_b, t):
    n = x.shape[0]
    tm = min(256, n)
    ec = PEER_CI * PEER_NKEYS
    row = lambda i, c: (i, 0)
    fixed = lambda i, c: (0, 0)
    last = n // tm - 1
    nxt = lambda i, c: (jnp.minimum(i + 1, last), 0)
    mod, mod_spec = _mod_spec(mod, t, tm, n)
    assert tm % (GATE_TOK * (PEER_NKEYS // PEER_CI)) == 0
    return pl.pallas_call(
        functools.partial(_peer_b_kernel, tm=tm),
        grid=(n // tm, PEER_NKEYS // PEER_CI),
        in_specs=[
            pl.BlockSpec((tm, D_MODEL), row),
            pl.BlockSpec((tm, PEER_SLOTS), row),
            pl.BlockSpec((tm, PEER_SLOTS), row),
            pl.BlockSpec((tm, PEER_SLOTS), nxt),
            pl.BlockSpec((tm, PEER_SLOTS), nxt),
            pl.BlockSpec((None, D_MODEL, ec), lambda i, c: (c, 0, 0)),
            pl.BlockSpec((ec, D_MODEL), lambda i, c: (c, 0)),
            pl.BlockSpec((tm, D_MODEL), row),
            mod_spec,
            pl.BlockSpec((1, D_MODEL), fixed),
            pl.BlockSpec((1, D_MODEL), fixed),
        ],
        out_specs=pl.BlockSpec((tm, D_MODEL), row),
        out_shape=jax.ShapeDtypeStruct((n, D_MODEL), F32),
        scratch_shapes=[pltpu.VMEM((2, PEER_NKEYS, tm, PEER_NKEYS), BF16), pltpu.VMEM((tm, D_MODEL), F32)],
        compiler_params=_cparams("arbitrary", "arbitrary"),
        name="peer_mix",
    )(h2, ids, gates, ids, gates, ut, vv, x, mod, ln_g.reshape(1, D_MODEL), ln_b.reshape(1, D_MODEL))


def _run_group(x3, mod_g, past_k, past_v, gdn_state, conv_state, wts):
    nb, t, _ = x3.shape
    n = nb * t
    cached = past_k is not None
    x = x3.reshape(n, D_MODEL)
    tp = -(-t // CHUNK) * CHUNK
    new_k, new_v, new_s, new_conv, new_sgu_v = [], [], [], [], []
    for l in range(DEPTH):
        mod = mod_g[l]
        pa, q, k, v, pc, pab = in_proj(x, mod, wts["w_main"][l], wts["w_ab"][l], t)
        ya, va = sgu(pa, wts["sgu_ln_g"][l], wts["sgu_ln_b"][l], wts["sgu_w"][l], wts["sgu_b"][l], t)
        lam_init = 0.8 - 0.6 * math.exp(-0.3 * l)
        lam = (jnp.exp(jnp.sum(wts["lam_q1"][l] * wts["lam_k1"][l]))
               - jnp.exp(jnp.sum(wts["lam_q2"][l] * wts["lam_k2"][l])) + lam_init)
        q3, k3, v3 = (a.reshape(nb, t, B_WIDTH) for a in (q, k, v))
        if cached:
            p_len = past_k.shape[2]
            tk = 256
            klen = p_len + t
            padk = -klen % tk
            kk = jnp.concatenate([past_k[l].reshape(nb, p_len, B_QK), k3, jnp.zeros((nb, padk, B_QK), F32)], axis=1)
            vv = jnp.concatenate([past_v[l].reshape(nb, p_len, B_WIDTH), v3, jnp.zeros((nb, padk, B_WIDTH), F32)], axis=1)
            yb = diff_attn(q3, kk, vv, lam, wts["diff_norm_g"][l], 1.0 - lam_init, q_off=p_len, klen=klen, tq=t, tk=tk)
        else:
            yb = diff_attn(q3, k3, v3, lam, wts["diff_norm_g"][l], 1.0 - lam_init, q_off=0, klen=t, tq=min(256, t), tk=min(512, t))
        pc3 = pc.reshape(nb, t, 1024)
        pab3 = pab.reshape(nb, t, LANES)
        if tp != t:
            pc3p = jnp.pad(pc3, ((0, 0), (0, tp - t), (0, 0)))
            pab3p = jnp.pad(pab3, ((0, 0), (0, tp - t), (0, 0)))
        else:
            pc3p, pab3p = pc3, pab3
        cst = conv_state[l] if cached else jnp.zeros((nb, CONV_W - 1, C_QKV), F32)
        s0 = gdn_state[l] if cached else jnp.zeros((nb, C_HEADS, C_DK, C_DV), F32)
        yc, s_fin = gdn(pc3p, pab3p, cst, s0, wts["conv_w"][l], wts["gdn_a_log"][l], wts["gdn_dt_bias"][l],
                        wts["gdn_norm_g"][l], t)
        yc = yc[:, :t].reshape(n, C_WIDTH)
        x = out_proj(x, ya, yb.reshape(n, B_WIDTH), yc, mod, wts["w_out"][l], wts["ln1_g"][l], wts["ln1_b"][l], t)
        h2, ids, gates = peer_retrieve(x, mod, wts["peer_wq"][l], wts["peer_keys"][l], t)
        x = peer_mix(h2, ids.T, gates.T, wts["expert_ut"][l], wts["expert_v"][l], x, mod,
                     wts["ln2_g"][l], wts["ln2_b"][l], t)
        new_k.append(k3.reshape(nb, t, B_HEADS, 2 * B_DK))
        new_v.append(v3.reshape(nb, t, B_HEADS, B_DV))
        new_s.append(s_fin)
        xp_tail = jnp.concatenate([cst, pc3[:, :, :C_QKV]], axis=1)[:, -(CONV_W - 1):] if t < CONV_W - 1 \
            else pc3[:, t - (CONV_W - 1):, :C_QKV]
        new_conv.append(xp_tail)
        new_sgu_v.append(va.reshape(nb, t, A_WIDTH))
    return (x.reshape(nb, t, D_MODEL), jnp.stack(new_k), jnp.stack(new_v), jnp.stack(new_s), jnp.stack(new_conv),
            jnp.stack(new_sgu_v))


def kernel(x_prompt, x_sample, cache_k, cache_v, state_gdn, state_conv, c_prompt, c_sample,
           w_ada, b_ada, w_in, sgu_ln_g, sgu_ln_b, sgu_w, sgu_b,
           lam_q1, lam_k1, lam_q2, lam_k2, diff_norm_g,
           conv_w, gdn_a_log, gdn_dt_bias, gdn_norm_g,
           w_out, ln1_g, ln1_b, peer_wq, peer_keys, expert_u, expert_v, ln2_g, ln2_b):
    nbp = x_prompt.shape[0]
    nbs = x_sample.shape[0]
    wts = dict(
        w_main=w_in[:, :, :P_MAIN].astype(BF16),
        w_ab=jnp.pad(w_in[:, :, P_MAIN:], ((0, 0), (0, 0), (0, LANES - (P_IN - P_MAIN)))).astype(BF16),
        sgu_ln_g=sgu_ln_g, sgu_ln_b=sgu_ln_b, sgu_w=sgu_w, sgu_b=sgu_b,
        lam_q1=lam_q1, lam_k1=lam_k1, lam_q2=lam_q2, lam_k2=lam_k2, diff_norm_g=diff_norm_g,
        conv_w=conv_w, gdn_a_log=gdn_a_log, gdn_dt_bias=gdn_dt_bias, gdn_norm_g=gdn_norm_g,
        w_out=w_out.astype(BF16), ln1_g=ln1_g, ln1_b=ln1_b,
        peer_wq=peer_wq.astype(BF16),
        peer_keys=peer_keys.reshape(DEPTH, 2 * PEER_HEADS, PEER_NKEYS, PEER_DQ // 2).astype(BF16),
        expert_ut=jnp.swapaxes(expert_u.astype(BF16).reshape(DEPTH, PEER_NKEYS // PEER_CI, PEER_CI * PEER_NKEYS, D_MODEL),
                               2, 3),
        expert_v=expert_v.astype(BF16),
        ln2_g=ln2_g, ln2_b=ln2_b,
    )
    mod_all = ada_mod(jnp.concatenate([c_prompt, c_sample], axis=0), w_ada, b_ada)
    mod_all = mod_all.reshape(DEPTH, nbp + nbs, 6, D_MODEL)
    y_p, k_p, v_p, s_p, conv_p, _ = _run_group(x_prompt, mod_all[:, :nbp], None, None, None, None, wts)
    y_s, k_s, v_s, s_s, conv_s, sgu_s = _run_group(x_sample, mod_all[:, nbp:], cache_k, cache_v, state_gdn, state_conv, wts)
    return (y_p, y_s, k_p, v_p, s_p, conv_p, k_s, v_s, s_s, conv_s, sgu_s)
```

```python
import functools
import math

import jax
import jax.numpy as jnp
from jax import lax
from jax.experimental import pallas as pl
from jax.experimental.pallas import tpu as pltpu

F32 = jnp.float32
BF16 = jnp.bfloat16

D_MODEL = 1024
DEPTH = 4
CHUNK = 64
A_GROUPS = 4
A_GDIM = 64
A_WIDTH = 256
SGU_LEN = 128
B_HEADS = 4
B_DK = 64
B_DV = 128
B_WIDTH = 512
B_QK = 512
C_HEADS = 4
C_DK = 64
C_DV = 64
C_WIDTH = 256
CONV_W = 4
C_QKV = 768
P_MAIN = 3072
P_IN = 3080
PEER_HEADS = 8
PEER_NKEYS = 128
PEER_DQ = 256
PEER_TOPK = 16
PEER_SLOTS = PEER_HEADS * PEER_TOPK
DN_ALPHA = (2 * DEPTH) ** 0.25
LN_EPS = 1e-5
NEG = -1e30

LANES = 128
VMEM_LIMIT = 56 * 1024 * 1024

NN = (((1,), (0,)), ((), ()))
NT = (((1,), (1,)), ((), ()))


def _cparams(*sem):
    return pltpu.CompilerParams(dimension_semantics=sem, vmem_limit_bytes=VMEM_LIMIT)


def _dot(a, b, dims=NN):
    return lax.dot_general(a, b, dims, preferred_element_type=F32)


def _split(x, n):
    parts = []
    r = x
    for _ in range(n - 1):
        p = r.astype(BF16)
        parts.append(p)
        r = r - p.astype(F32)
    parts.append(r.astype(BF16))
    return parts


def _dot3(a, b, dims=NN):
    ah, al = _split(a, 2)
    bh, bl = _split(b, 2)
    return _dot(ah, bh, dims) + (_dot(ah, bl, dims) + _dot(al, bh, dims))


def _dot_exact_rhs(x, m_bf16, n=3, dims=NN):
    parts = _split(x, n)
    acc = _dot(parts[-1], m_bf16, dims)
    for p in parts[-2::-1]:
        acc = acc + _dot(p, m_bf16, dims)
    return acc


def _dot_exact_lhs(m_bf16, x, n=3, dims=NN):
    parts = _split(x, n)
    acc = _dot(m_bf16, parts[-1], dims)
    for p in parts[-2::-1]:
        acc = acc + _dot(m_bf16, p, dims)
    return acc


def _ln(x):
    mu = jnp.mean(x, axis=-1, keepdims=True)
    d = x - mu
    var = jnp.mean(d * d, axis=-1, keepdims=True)
    return d * lax.rsqrt(var + LN_EPS)


def _iota(shape, dim):
    return lax.broadcasted_iota(jnp.int32, shape, dim)


def _ada_kernel(c_ref, w_ref, b_ref, o_ref):
    c = c_ref[...]
    s = c * jax.nn.sigmoid(c)
    o_ref[0] = _dot3(s, w_ref[0]) + b_ref[0]


def ada_mod(c_all, w_ada, b_ada):
    nbt = c_all.shape[0]
    tn = 1536
    return pl.pallas_call(
        _ada_kernel,
        grid=(DEPTH, 6 * D_MODEL // tn),
        in_specs=[
            pl.BlockSpec((nbt, D_MODEL), lambda l, j: (0, 0)),
            pl.BlockSpec((1, D_MODEL, tn), lambda l, j: (l, 0, j)),
            pl.BlockSpec((1, 1, tn), lambda l, j: (l, 0, j)),
        ],
        out_specs=pl.BlockSpec((1, nbt, tn), lambda l, j: (l, 0, j)),
        out_shape=jax.ShapeDtypeStruct((DEPTH, nbt, 6 * D_MODEL), F32),
        compiler_params=_cparams("parallel", "parallel"),
        name="ada_mod",
    )(c_all, w_ada, b_ada.reshape(DEPTH, 1, 6 * D_MODEL))


def _in_kernel(x_ref, mod_ref, w_ref, wab_ref, pa_ref, q_ref, k_ref, v_ref, pc_ref, pab_ref):
    h = _ln(x_ref[...]) * (1.0 + mod_ref[0, 1:2, :]) + mod_ref[0, 0:1, :]
    hb = h.astype(BF16)
    pa_ref[...] = _dot(hb, w_ref[:, 0:512])
    q_ref[...] = _dot(hb, w_ref[:, 512:1024])
    k_ref[...] = _dot(hb, w_ref[:, 1024:1536])
    v_ref[...] = _dot(hb, w_ref[:, 1536:2048])
    pc_ref[...] = _dot(hb, w_ref[:, 2048:3072])
    pab_ref[...] = _dot(hb, wab_ref[...])


def in_proj(x, mod, w_main, w_ab, t):
    n = x.shape[0]
    tm = min(512, t)
    per = t // tm
    row = lambda i: (i, 0)
    fixed = lambda i: (0, 0)
    sds = lambda w: jax.ShapeDtypeStruct((n, w), F32)
    return pl.pallas_call(
        _in_kernel,
        grid=(n // tm,),
        in_specs=[
            pl.BlockSpec((tm, D_MODEL), row),
            pl.BlockSpec((1, 6, D_MODEL), lambda i: (i // per, 0, 0)),
            pl.BlockSpec((D_MODEL, P_MAIN), fixed),
            pl.BlockSpec((D_MODEL, LANES), fixed),
        ],
        out_specs=[pl.BlockSpec((tm, 512), row)] * 4 + [pl.BlockSpec((tm, 1024), row), pl.BlockSpec((tm, LANES), row)],
        out_shape=[sds(512)] * 4 + [sds(1024), sds(LANES)],
        compiler_params=_cparams("parallel"),
        name="in_proj",
    )(x, mod, w_main, w_ab)


def _group_ones(width, gdim):
    r = _iota((width, width), 0) // gdim
    c = _iota((width, width), 1) // gdim
    return jnp.where(r == c, 1.0, 0.0).astype(BF16)


def _sgu_kernel(pa_ref, g_ref, b_ref, w_ref, bs_ref, ya_ref, va_ref, *, blk, nblk):
    pa = pa_ref[...]
    u = jax.nn.gelu(pa[:, :A_WIDTH])
    v = jax.nn.gelu(pa[:, A_WIDTH:])
    ones = _group_ones(A_WIDTH, A_GDIM)
    mean = _dot_exact_rhs(v, ones) * (1.0 / A_GDIM)
    d = v - mean
    var = _dot_exact_rhs(d * d, ones) * (1.0 / A_GDIM)
    vn = d * lax.rsqrt(var + LN_EPS) * g_ref[...] + b_ref[...]
    va_ref[...] = vn
    ri = _iota((blk, blk), 0) // CHUNK
    ci = _iota((blk, blk), 1) // CHUNK
    lane_group = _iota((blk, A_WIDTH), 1) // A_GDIM
    wm = [jnp.where(ci <= ri, w_ref[g], 0.0).astype(BF16) for g in range(A_GROUPS)]
    for r in range(nblk):
        vb = vn[r * blk:(r + 1) * blk, :].astype(BF16)
        s = jnp.zeros((blk, A_WIDTH), F32)
        for g in range(A_GROUPS):
            s = jnp.where(lane_group == g, _dot(wm[g], vb), s)
        ya_ref[r * blk:(r + 1) * blk, :] = u[r * blk:(r + 1) * blk, :] * (s + bs_ref[...])


def sgu(pa, ln_g, ln_b, w_s, b_s, t):
    n = pa.shape[0]
    blk = min(SGU_LEN, t)
    assert t % blk == 0
    tm = min(512, n)
    w = w_s[:, :blk, :blk]
    bs = jnp.repeat(b_s.T[:blk], A_GDIM, axis=1)
    row = lambda i: (i, 0)
    return pl.pallas_call(
        functools.partial(_sgu_kernel, blk=blk, nblk=tm // blk),
        grid=(n // tm,),
        in_specs=[
            pl.BlockSpec((tm, 2 * A_WIDTH), row),
            pl.BlockSpec((1, A_WIDTH), lambda i: (0, 0)),
            pl.BlockSpec((1, A_WIDTH), lambda i: (0, 0)),
            pl.BlockSpec((A_GROUPS, blk, blk), lambda i: (0, 0, 0)),
            pl.BlockSpec((blk, A_WIDTH), lambda i: (0, 0)),
        ],
        out_specs=[pl.BlockSpec((tm, A_WIDTH), row)] * 2,
        out_shape=[jax.ShapeDtypeStruct((n, A_WIDTH), F32)] * 2,
        compiler_params=_cparams("parallel"),
        name="sgu",
    )(pa, ln_g.reshape(1, A_WIDTH), ln_b.reshape(1, A_WIDTH), w, bs)


ATTN_HEADS_PER_STEP = 2


def _attn_kernel(lam_ref, slope_ref, q_ref, k_ref, v_ref, g_ref, o_ref, *, tq, tk, q_off, klen, out_scale, nh):
    hg = pl.program_id(1)
    qi = pl.program_id(2)
    rows = 2 * tq
    log2e = 1.4426950408889634
    lane = _iota((tq, B_DV), 1)
    q_lo = q_off + qi * tq
    qpos = q_lo + (_iota((rows, tk), 0) & (tq - 1))
    col = _iota((rows, tk), 1)
    colf = _iota((1, tk), 1).astype(F32)
    limit = jnp.minimum(klen, ((q_lo + tq - 1) // CHUNK + 1) * CHUNK)
    nkb = (limit + tk - 1) // tk
    nfull = jnp.minimum(q_lo, klen) // tk
    heads = range(nh)
    hl = [slice(e * B_DV, (e + 1) * B_DV) for e in heads]
    slope2 = [slope_ref[hg * nh + e] * log2e for e in heads]
    qq = []
    for e in heads:
        q = q_ref[0, :, hl[e]] * (B_DK ** -0.5 * log2e)
        qq.append(jnp.concatenate([jnp.where(lane < B_DK, q, 0.0), jnp.where(lane >= B_DK, q, 0.0)],
                                  axis=0).astype(BF16))

    def scores(e, kb):
        start = pl.multiple_of(kb * tk, tk)
        return _dot(qq[e], k_ref[0, pl.ds(start, tk), hl[e]].astype(BF16), NT)

    def update(e, kb, s, m, l, acc):
        start = pl.multiple_of(kb * tk, tk)
        vblk = v_ref[0, pl.ds(start, tk), hl[e]].astype(BF16)
        m_new = jnp.maximum(m, jnp.max(s, axis=1, keepdims=True))
        alpha = jnp.exp2(m - m_new)
        p = jnp.exp2(s - m_new)
        l = alpha * l + jnp.sum(p, axis=1, keepdims=True)
        acc = alpha * acc + _dot(p.astype(BF16), vblk)
        return m_new, l, acc

    def masked(e, kb, s):
        kpos = kb * tk + col
        s = s + slope2[e] * (qpos - jnp.abs(qpos - kpos)).astype(F32)
        kchunk = jnp.where(kpos < klen, kpos // CHUNK, jnp.int32(1 << 30))
        return jnp.where(kchunk <= (qpos // CHUNK), s, NEG)

    def full_body(kb, carry):
        ss = [scores(e, kb) + slope2[e] * ((kb * tk).astype(F32) + colf) for e in heads]
        return tuple(update(e, kb, ss[e], *carry[e]) for e in heads)

    def tail_body(kb, carry):
        ss = [masked(e, kb, scores(e, kb)) for e in heads]
        return tuple(update(e, kb, ss[e], *carry[e]) for e in heads)

    init = tuple((jnp.full((rows, 1), NEG, F32), jnp.zeros((rows, 1), F32), jnp.zeros((rows, B_DV), F32))
                 for e in heads)
    carry = lax.fori_loop(0, nfull, full_body, init)
    carry = lax.fori_loop(nfull, nkb, tail_body, carry)
    for e in heads:
        m, l, acc = carry[e]
        o = acc / l
        ob = o[:tq] - lam_ref[0] * o[tq:]
        ms = jnp.mean(ob * ob, axis=-1, keepdims=True)
        o_ref[0, :, hl[e]] = ob * lax.rsqrt(ms + LN_EPS) * g_ref[...] * out_scale


def diff_attn(q, k, v, lam, norm_g, out_scale, *, q_off, klen, tq, tk):
    nb, t, _ = q.shape
    tkp = k.shape[1]
    assert klen >= q_off + t and tkp % tk == 0 and tkp >= klen and t % tq == 0
    slopes = jnp.exp2(-(8.0 / B_HEADS) * jnp.arange(1, B_HEADS + 1, dtype=F32))
    smem = pl.BlockSpec(memory_space=pltpu.SMEM)
    nh = ATTN_HEADS_PER_STEP
    return pl.pallas_call(
        functools.partial(_attn_kernel, tq=tq, tk=tk, q_off=q_off, klen=klen, out_scale=out_scale, nh=nh),
        grid=(nb, B_HEADS // nh, t // tq),
        in_specs=[
            smem, smem,
            pl.BlockSpec((1, tq, nh * B_DV), lambda b, h, i: (b, i, h)),
            pl.BlockSpec((1, tkp, nh * B_DV), lambda b, h, i: (b, 0, h)),
            pl.BlockSpec((1, tkp, nh * B_DV), lambda b, h, i: (b, 0, h)),
            pl.BlockSpec((1, B_DV), lambda b, h, i: (0, 0)),
        ],
        out_specs=pl.BlockSpec((1, tq, nh * B_DV), lambda b, h, i: (b, i, h)),
        out_shape=jax.ShapeDtypeStruct((nb, t, B_WIDTH), F32),
        compiler_params=_cparams("parallel", "parallel", "arbitrary"),
        name="diff_attn",
    )(lam.reshape(1), slopes, q, k, v, norm_g.reshape(1, B_DV))


GDN_ROWS = 8


def _gdn_kernel(pc_ref, pab_ref, cst_ref, cw_ref, alog_ref, dtb_ref, ng_ref, s0_ref,
                yc_ref, sfin_ref, prev_scr, *, t_valid, bb):
    c = pl.program_id(1)

    @pl.when(c == 0)
    def _():
        prev_scr[...] = cst_ref[...]
        sfin_ref[...] = s0_ref[...]

    rows = [_gdn_chunk(bi, c, pc_ref, pab_ref, cw_ref, alog_ref, dtb_ref, ng_ref, yc_ref, sfin_ref, prev_scr, t_valid)
            for bi in range(bb)]
    while rows:
        rows = [r for r in rows if next(r, True) is None]


def _gdn_chunk(bi, c, pc_ref, pab_ref, cw_ref, alog_ref, dtb_ref, ng_ref, yc_ref, sfin_ref, prev_scr, t_valid):
    W = C_WIDTH
    qkv = pc_ref[bi, :, 0:C_QKV]
    z = pc_ref[bi, :, C_QKV:C_QKV + W]
    ext = jnp.concatenate([prev_scr[bi], qkv], axis=0)
    prev_scr[bi] = qkv[CHUNK - 8:, :]
    conv = ext[5:5 + CHUNK] * cw_ref[0:1, :]
    for j in range(1, CONV_W):
        conv = conv + ext[5 + j:5 + j + CHUNK] * cw_ref[j:j + 1, :]
    conv = conv * jax.nn.sigmoid(conv)
    q, k, v = conv[:, 0:W], conv[:, W:2 * W], conv[:, 2 * W:3 * W]

    head_ones = _group_ones(W, C_DK)
    qss = _dot_exact_rhs(q * q, head_ones)
    kss = _dot_exact_rhs(k * k, head_ones)
    pab = pab_ref[bi]
    g_cols = -jnp.exp(alog_ref[...]) * jax.nn.softplus(pab + dtb_ref[...])
    b_cols = jax.nn.sigmoid(pab)
    er = _iota((LANES, W), 0)
    eh = _iota((LANES, W), 1) // C_DK
    g = _dot_exact_rhs(g_cols, jnp.where(er == eh, 1.0, 0.0).astype(BF16))
    beta = _dot_exact_rhs(b_cols, jnp.where(er == eh + C_HEADS, 1.0, 0.0).astype(BF16))
    yield
    q = q * lax.rsqrt(qss + 1e-6) * (C_DK ** -0.5)
    k = k * lax.rsqrt(kss + 1e-6)
    if t_valid % CHUNK:
        valid = (c * CHUNK + _iota((CHUNK, W), 0)) < t_valid
        q, k, v = (jnp.where(valid, a, 0.0) for a in (q, k, v))
        g = jnp.where(valid, g, 0.0)
        beta = jnp.where(valid, beta, 0.0)

    tri = jnp.where(_iota((CHUNK, CHUNK), 0) >= _iota((CHUNK, CHUNK), 1), 1.0, 0.0).astype(BF16)
    gc = _dot_exact_lhs(tri, g)
    yield
    eg = jnp.exp(gc)
    glast = gc[CHUNK - 1:CHUNK, :]
    kdec = k * jnp.exp(glast - gc)
    qg = q * eg
    kb = k * beta
    kbe = kb * eg
    vb = v * beta

    R = C_HEADS * CHUNK
    same = (_iota((R, W), 0) // CHUNK) == (_iota((R, W), 1) // C_DK)
    ii = _iota((CHUNK, W), 0)
    jj = _iota((CHUNK, W), 1) % CHUNK

    same16 = jnp.where(same, 1.0, 0.0).astype(BF16)

    def bd(a):
        return jnp.where(same, jnp.concatenate([a] * C_HEADS, axis=0), 0.0)

    def bd16(a16):
        return jnp.concatenate([a16] * C_HEADS, axis=0) * same16

    def split_bd(a):
        hi = a.astype(BF16)
        lo = (a - hi.astype(F32)).astype(BF16)
        return hi, lo, bd16(hi), bd16(lo)

    def mm3(a_hi, a_lo, b_hi, b_lo):
        return _dot(a_hi, b_hi) + (_dot(a_hi, b_lo) + _dot(a_lo, b_hi))

    kx = bd16(k.astype(BF16))
    kk = _dot(kb.astype(BF16), kx, NT)
    qk = _dot(q.astype(BF16), kx, NT)
    pick = jnp.where(jj == 0, 1.0, 0.0).astype(BF16)
    gparts = _split(gc, 3)
    g_col = _dot(pick, bd16(gparts[2]), NT)
    g_col = g_col + _dot(pick, bd16(gparts[1]), NT)
    g_col = g_col + _dot(pick, bd16(gparts[0]), NT)
    yield
    lower = ii >= jj
    gamma = jnp.where(lower, jnp.exp(jnp.where(lower, gc - g_col, 0.0)), 0.0)
    nm = jnp.where(ii > jj, -(kk * gamma), 0.0)
    aqk = qk * gamma

    tinv = jnp.where(ii == jj, 1.0, 0.0) + nm
    p_hi, p_lo, pb_hi, pb_lo = split_bd(nm)
    for step in range(5):
        pw = mm3(p_hi, p_lo, pb_hi, pb_lo)
        yield
        p_hi, p_lo, pb_hi, pb_lo = split_bd(pw)
        t_hi, t_lo = _split(tinv, 2)
        tinv = tinv + mm3(t_hi, t_lo, pb_hi, pb_lo)
    _, _, vb_hi, vb_lo = split_bd(vb)
    _, _, ke_hi, ke_lo = split_bd(kbe)
    yield
    t_hi, t_lo = _split(tinv, 2)
    u = mm3(t_hi, t_lo, vb_hi, vb_lo)
    w = mm3(t_hi, t_lo, ke_hi, ke_lo)
    s = sfin_ref[bi]
    sb = s.astype(BF16)
    oq = _dot(qg.astype(BF16), sb)
    kdt = kdec.T.astype(BF16)
    yield
    v_new = u - _dot(w.astype(BF16), sb)
    yield
    vnb = v_new.astype(BF16)
    o = oq + _dot(aqk.astype(BF16), bd16(vnb))
    sfin_ref[bi] = s * jnp.exp(glast) + jnp.where(same, _dot(kdt, vnb), 0.0)
    yield
    ms = _dot_exact_rhs(o * o, head_ones) * (1.0 / C_DV)
    yield
    yc_ref[bi] = o * lax.rsqrt(ms + LN_EPS) * ng_ref[...] * (z * jax.nn.sigmoid(z))


def gdn(pc, pab, conv_state, s0, conv_w, a_log, dt_bias, norm_g, t_valid):
    nb, tp, _ = pc.shape
    cst = jnp.pad(conv_state, ((0, 0), (8 - (CONV_W - 1), 0), (0, 0)))
    cw = jnp.pad(conv_w, ((0, 8 - CONV_W), (0, 0)))
    pad_row = lambda a: jnp.pad(a.reshape(1, C_HEADS), ((0, 0), (0, LANES - C_HEADS)))
    eye = jnp.eye(C_HEADS, dtype=F32)
    s0_bd = jnp.einsum('bhde,hg->bhdge', s0, eye).reshape(nb, C_HEADS * C_DK, C_HEADS * C_DV)
    full = lambda b, c: (0, 0)
    bb = math.gcd(nb, GDN_ROWS)
    yc, sfin = pl.pallas_call(
        functools.partial(_gdn_kernel, t_valid=t_valid, bb=bb),
        grid=(nb // bb, tp // CHUNK),
        in_specs=[
            pl.BlockSpec((bb, CHUNK, 1024), lambda b, c: (b, c, 0)),
            pl.BlockSpec((bb, CHUNK, LANES), lambda b, c: (b, c, 0)),
            pl.BlockSpec((bb, 8, C_QKV), lambda b, c: (b, 0, 0)),
            pl.BlockSpec((8, C_QKV), full),
            pl.BlockSpec((1, LANES), full),
            pl.BlockSpec((1, LANES), full),
            pl.BlockSpec((1, C_WIDTH), full),
            pl.BlockSpec((bb, C_WIDTH, C_WIDTH), lambda b, c: (b, 0, 0)),
        ],
        out_specs=[
            pl.BlockSpec((bb, CHUNK, C_WIDTH), lambda b, c: (b, c, 0)),
            pl.BlockSpec((bb, C_WIDTH, C_WIDTH), lambda b, c: (b, 0, 0)),
        ],
        out_shape=[
            jax.ShapeDtypeStruct((nb, tp, C_WIDTH), F32),
            jax.ShapeDtypeStruct((nb, C_WIDTH, C_WIDTH), F32),
        ],
        scratch_shapes=[pltpu.VMEM((bb, 8, C_QKV), F32)],
        compiler_params=_cparams("parallel", "arbitrary"),
        name="gdn",
    )(pc, pab, cst, cw, pad_row(a_log), pad_row(dt_bias), jnp.tile(norm_g.reshape(1, C_DV), (1, C_HEADS)), s0_bd)
    s5 = sfin.reshape(nb, C_HEADS, C_DK, C_HEADS, C_DV)
    s_fin = jnp.stack([s5[:, h, :, h, :] for h in range(C_HEADS)], axis=1)
    return yc, s_fin


def _out_kernel(x_ref, ya_ref, yb_ref, yc_ref, mod_ref, w_ref, g_ref, b_ref, o_ref):
    y = _dot(ya_ref[...].astype(BF16), w_ref[0:256, :])
    y = y + _dot(yb_ref[...].astype(BF16), w_ref[256:768, :])
    y = y + _dot(yc_ref[...].astype(BF16), w_ref[768:1024, :])
    r = DN_ALPHA * x_ref[...] + mod_ref[0, 2:3, :] * y
    o_ref[...] = _ln(r) * g_ref[...] + b_ref[...]


def out_proj(x, ya, yb, yc, mod, w_out, ln_g, ln_b, t):
    n = x.shape[0]
    tm = min(512, t)
    per = t // tm
    row = lambda i: (i, 0)
    fixed = lambda i: (0, 0)
    return pl.pallas_call(
        _out_kernel,
        grid=(n // tm,),
        in_specs=[
            pl.BlockSpec((tm, D_MODEL), row),
            pl.BlockSpec((tm, A_WIDTH), row),
            pl.BlockSpec((tm, B_WIDTH), row),
            pl.BlockSpec((tm, C_WIDTH), row),
            pl.BlockSpec((1, 6, D_MODEL), lambda i: (i // per, 0, 0)),
            pl.BlockSpec((D_MODEL, D_MODEL), fixed),
            pl.BlockSpec((1, D_MODEL), fixed),
            pl.BlockSpec((1, D_MODEL), fixed),
        ],
        out_specs=pl.BlockSpec((tm, D_MODEL), row),
        out_shape=jax.ShapeDtypeStruct((n, D_MODEL), F32),
        compiler_params=_cparams("parallel"),
        name="out_proj",
    )(x, ya, yb, yc, mod, w_out, ln_g.reshape(1, D_MODEL), ln_b.reshape(1, D_MODEL))


def _round_robin(gens):
    gens = list(gens)
    while gens:
        gens = [g for g in gens if next(g, True) is None]


def _extract_top(vals, order, out_s_ref, out_p_ref, base, lanes):
    big = 1e9
    for j in range(PEER_TOPK):
        m = jnp.max(vals, axis=0, keepdims=True)
        tied = jnp.where(vals == m, order, big)
        pos = jnp.min(tied, axis=0, keepdims=True)
        out_s_ref[base + j:base + j + 1, lanes] = m
        out_p_ref[base + j:base + j + 1, lanes] = pos
        vals = jnp.where(tied == pos, -jnp.inf, vals)
        yield


def _extract_top_keys(scores, out_s_ref, out_i_ref, base, lanes):
    half = PEER_NKEYS // 2
    a, b = scores[0:half], scores[half:PEER_NKEYS]
    ia = _iota((half, LANES), 0).astype(F32)
    ib = ia + float(half)
    swap = b > a
    top, rest = jnp.where(swap, b, a), jnp.where(swap, a, b)
    itop, irest = jnp.where(swap, ib, ia), jnp.where(swap, ia, ib)
    big = 1e9
    for j in range(PEER_TOPK):
        m = jnp.max(top, axis=0, keepdims=True)
        tied = jnp.where(top == m, itop, big)
        pos = jnp.min(tied, axis=0, keepdims=True)
        out_s_ref[base + j:base + j + 1, lanes] = m
        out_i_ref[base + j:base + j + 1, lanes] = pos
        hit = tied == pos
        top = jnp.where(hit, rest, top)
        itop = jnp.where(hit, irest, itop)
        rest = jnp.where(hit, -jnp.inf, rest)
        yield


def _mod_row(mod_ref, i):
    return mod_ref[0, i:i + 1, :] if len(mod_ref.shape) == 3 else mod_ref[0, i]


def _mod_spec(mod, t, tm, n):
    if tm > t:
        mod = jnp.repeat(mod, t, axis=0).reshape(n // tm, tm, 6, D_MODEL).transpose(0, 2, 1, 3)
        return mod, pl.BlockSpec((1, 6, tm, D_MODEL), lambda i, *_: (i, 0, 0, 0))
    per = t // tm
    return mod, pl.BlockSpec((1, 6, D_MODEL), lambda i, *_: (i // per, 0, 0))


def _peer_a_kernel(x_ref, mod_ref, wq_ref, keys_ref, h2_ref, ids_ref, gates_ref, ts_scr, ti_scr, bs_scr, id_scr):
    tm = x_ref.shape[0]
    h2 = _ln(x_ref[...]) * (1.0 + _mod_row(mod_ref, 4)) + _mod_row(mod_ref, 3)
    hb = h2.astype(BF16)
    h2_ref[...] = hb
    groups = [slice(g * LANES, (g + 1) * LANES) for g in range(tm // LANES)]
    for hp in range(2 * PEER_HEADS):
        qhp = _dot(hb, wq_ref[:, hp * LANES:(hp + 1) * LANES]).astype(BF16)
        st = _dot(keys_ref[hp], qhp, NT)
        _round_robin(_extract_top_keys(st[:, lanes], ts_scr, ti_scr, hp * PEER_TOPK, lanes) for lanes in groups)
    r8i = _iota((8, LANES), 0)
    r8 = r8i.astype(F32)
    for h in range(PEER_HEADS):
        b0 = 2 * h * PEER_TOPK
        b1 = b0 + PEER_TOPK
        rounds = []
        for lanes in groups:
            cand, order = [], []

            def add(sa, sb, rows_a, lo, hi):
                va = ts_scr[b0 + sa[0]:b0 + sa[0] + sa[1], lanes]
                vb = ts_scr[b1 + sb[0]:b1 + sb[0] + sb[1], lanes]
                s = va + vb
                if (lo, hi) != (0, 8):
                    s = jnp.where(r8i >= lo, jnp.where(r8i < hi, s, -jnp.inf), -jnp.inf)
                cand.append(s)
                order.append((sa[0] + r8) * float(PEER_TOPK) + sb[0] if rows_a else sa[0] * PEER_TOPK + sb[0] + r8)

            add((0, 1), (0, 8), False, 0, 8)
            add((0, 1), (8, 8), False, 0, 8)
            add((1, 1), (0, 8), False, 0, 8)
            add((8, 8), (0, 1), True, 0, 8)
            for b in range(5):
                add((0, 8), (b, 1), True, 2, PEER_TOPK // (b + 1))
            rounds.append(_extract_top(jnp.concatenate(cand, axis=0), jnp.concatenate(order, axis=0),
                                       bs_scr, id_scr, h * PEER_TOPK, lanes))
        _round_robin(rounds)
        rows = slice(h * PEER_TOPK, (h + 1) * PEER_TOPK)
        pick = id_scr[rows, :].astype(jnp.int32)
        pa, pb = pick >> 4, pick & (PEER_TOPK - 1)
        key1 = jnp.zeros((PEER_TOPK, tm), F32)
        key2 = jnp.zeros((PEER_TOPK, tm), F32)
        for r in range(PEER_TOPK):
            key1 = jnp.where(pa == r, ti_scr[b0 + r:b0 + r + 1, :], key1)
            key2 = jnp.where(pb == r, ti_scr[b1 + r:b1 + r + 1, :], key2)
        ids_ref[rows, :] = (key1 * float(PEER_NKEYS) + key2).astype(jnp.int32)
        bs = bs_scr[rows, :]
        e = jnp.exp(bs - bs[0:1, :])
        gates_ref[rows, :] = e / jnp.sum(e, axis=0, keepdims=True)


def peer_retrieve(x, mod, wq, keys, t):
    n = x.shape[0]
    tm = min(256, n)
    row = lambda i: (i, 0)
    col = lambda i: (0, i)
    mod, mod_spec = _mod_spec(mod, t, tm, n)
    return pl.pallas_call(
        _peer_a_kernel,
        grid=(n // tm,),
        in_specs=[
            pl.BlockSpec((tm, D_MODEL), row),
            mod_spec,
            pl.BlockSpec((D_MODEL, PEER_HEADS * PEER_DQ), lambda i: (0, 0)),
            pl.BlockSpec((2 * PEER_HEADS, PEER_NKEYS, PEER_DQ // 2), lambda i: (0, 0, 0)),
        ],
        out_specs=[pl.BlockSpec((tm, D_MODEL), row), pl.BlockSpec((PEER_SLOTS, tm), col), pl.BlockSpec((PEER_SLOTS, tm), col)],
        out_shape=[
            jax.ShapeDtypeStruct((n, D_MODEL), BF16),
            jax.ShapeDtypeStruct((PEER_SLOTS, n), jnp.int32),
            jax.ShapeDtypeStruct((PEER_SLOTS, n), F32),
        ],
        scratch_shapes=[
            pltpu.VMEM((2 * PEER_HEADS * PEER_TOPK, tm), F32),
            pltpu.VMEM((2 * PEER_HEADS * PEER_TOPK, tm), F32),
            pltpu.VMEM((PEER_SLOTS, tm), F32),
            pltpu.VMEM((PEER_SLOTS, tm), F32),
        ],
        compiler_params=_cparams("parallel"),
        name="peer_retrieve",
    )(x, mod, wq, keys)


PEER_CI = 16


GATE_TOK = 16


def _build_gates(ids_ref, gates_ref, base, gmat_scr, slot):
    sub = _iota((PEER_NKEYS, PEER_SLOTS), 0).astype(F32).astype(BF16)
    ids = ids_ref[pl.ds(base, GATE_TOK), :]
    gates = gates_ref[pl.ds(base, GATE_TOK), :].astype(BF16)
    key1 = (ids >> 7).astype(F32).astype(BF16)
    key2 = (ids & (PEER_NKEYS - 1)).astype(F32).astype(BF16)
    one = jnp.ones((PEER_NKEYS, PEER_SLOTS), BF16)
    zero = jnp.zeros((PEER_NKEYS, PEER_SLOTS), BF16)
    mats = []
    for r in range(GATE_TOK):
        a = jnp.where(sub == key1[r:r + 1, :], one, zero)
        b = jnp.where(sub == key2[r:r + 1, :], jnp.broadcast_to(gates[r:r + 1, :], zero.shape), zero)
        mats.append(_dot(a, b, NT))
    halves = [jnp.swapaxes(jnp.stack(mats[r:r + 8], axis=0), 0, 1) for r in range(0, GATE_TOK, 8)]
    g = jnp.concatenate(halves, axis=1)
    gmat_scr[slot, :, pl.ds(base, GATE_TOK), :] = g.astype(BF16)


def _peer_b_kernel(h2_ref, ids_ref, gates_ref, ids_next_ref, gates_next_ref, ut_ref, v_ref, x_ref, mod_ref,
                   g_ref, b_ref, o_ref, gmat_scr, acc_scr, *, tm):
    i = pl.program_id(0)
    c = pl.program_id(1)
    nch = PEER_NKEYS // PEER_CI
    slot = i % 2

    @pl.when(jnp.logical_and(i == 0, c == 0))
    def _():
        def group(n, carry):
            _build_gates(ids_ref, gates_ref, pl.multiple_of(n * GATE_TOK, GATE_TOK), gmat_scr, 0)
            return carry

        lax.fori_loop(0, tm // GATE_TOK, group, 0)

    act = _dot(h2_ref[...], ut_ref[...])
    ws = []
    for j in range(PEER_CI):
        gj = gmat_scr[slot, c * PEER_CI + j].astype(F32)
        ws.append((gj * jax.nn.gelu(act[:, j * LANES:(j + 1) * LANES])).astype(BF16))
    part = _dot(jnp.concatenate(ws, axis=1), v_ref[...])

    share = tm // nch
    for r in range(share // GATE_TOK):
        _build_gates(ids_next_ref, gates_next_ref, pl.multiple_of(c * share + r * GATE_TOK, GATE_TOK), gmat_scr, 1 - slot)

    @pl.when(c == 0)
    def _():
        acc_scr[...] = part

    @pl.when(c > 0)
    def _():
        acc_scr[...] += part

    @pl.when(c == pl.num_programs(1) - 1)
    def _():
        r = DN_ALPHA * x_ref[...] + _mod_row(mod_ref, 5) * acc_scr[...]
        o_ref[...] = _ln(r) * g_ref[...] + b_ref[...]


def peer_mix(h2, ids, gates, ut, vv, x, mod, ln_g, ln_b, t):
    n = x.shape[0]
    tm = min(256, n)
    ec = PEER_CI * PEER_NKEYS
    row = lambda i, c: (i, 0)
    fixed = lambda i, c: (0, 0)
    last = n // tm - 1
    nxt = lambda i, c: (jnp.minimum(i + 1, last), 0)
    mod, mod_spec = _mod_spec(mod, t, tm, n)
    assert tm % (GATE_TOK * (PEER_NKEYS // PEER_CI)) == 0
    return pl.pallas_call(
        functools.partial(_peer_b_kernel, tm=tm),
        grid=(n // tm, PEER_NKEYS // PEER_CI),
        in_specs=[
            pl.BlockSpec((tm, D_MODEL), row),
            pl.BlockSpec((tm, PEER_SLOTS), row),
            pl.BlockSpec((tm, PEER_SLOTS), row),
            pl.BlockSpec((tm, PEER_SLOTS), nxt),
            pl.BlockSpec((tm, PEER_SLOTS), nxt),
            pl.BlockSpec((None, D_MODEL, ec), lambda i, c: (c, 0, 0)),
            pl.BlockSpec((ec, D_MODEL), lambda i, c: (c, 0)),
            pl.BlockSpec((tm, D_MODEL), row),
            mod_spec,
            pl.BlockSpec((1, D_MODEL), fixed),
            pl.BlockSpec((1, D_MODEL), fixed),
        ],
        out_specs=pl.BlockSpec((tm, D_MODEL), row),
        out_shape=jax.ShapeDtypeStruct((n, D_MODEL), F32),
        scratch_shapes=[pltpu.VMEM((2, PEER_NKEYS, tm, PEER_NKEYS), BF16), pltpu.VMEM((tm, D_MODEL), F32)],
        compiler_params=_cparams("arbitrary", "arbitrary"),
        name="peer_mix",
    )(h2, ids, gates, ids, gates, ut, vv, x, mod, ln_g.reshape(1, D_MODEL), ln_b.reshape(1, D_MODEL))


def _run_group(x3, mod_g, past_k, past_v, gdn_state, conv_state, wts):
    nb, t, _ = x3.shape
    n = nb * t
    cached = past_k is not None
    x = x3.reshape(n, D_MODEL)
    tp = -(-t // CHUNK) * CHUNK
    new_k, new_v, new_s, new_conv, new_sgu_v = [], [], [], [], []
    for l in range(DEPTH):
        mod = mod_g[l]
        pa, q, k, v, pc, pab = in_proj(x, mod, wts["w_main"][l], wts["w_ab"][l], t)
        ya, va = sgu(pa, wts["sgu_ln_g"][l], wts["sgu_ln_b"][l], wts["sgu_w"][l], wts["sgu_b"][l], t)
        lam_init = 0.8 - 0.6 * math.exp(-0.3 * l)
        lam = (jnp.exp(jnp.sum(wts["lam_q1"][l] * wts["lam_k1"][l]))
               - jnp.exp(jnp.sum(wts["lam_q2"][l] * wts["lam_k2"][l])) + lam_init)
        q3, k3, v3 = (a.reshape(nb, t, B_WIDTH) for a in (q, k, v))
        if cached:
            p_len = past_k.shape[2]
            tk = 256
            klen = p_len + t
            padk = -klen % tk
            kk = jnp.concatenate([past_k[l].reshape(nb, p_len, B_QK), k3, jnp.zeros((nb, padk, B_QK), F32)], axis=1)
            vv = jnp.concatenate([past_v[l].reshape(nb, p_len, B_WIDTH), v3, jnp.zeros((nb, padk, B_WIDTH), F32)], axis=1)
            yb = diff_attn(q3, kk, vv, lam, wts["diff_norm_g"][l], 1.0 - lam_init, q_off=p_len, klen=klen, tq=t, tk=tk)
        else:
            yb = diff_attn(q3, k3, v3, lam, wts["diff_norm_g"][l], 1.0 - lam_init, q_off=0, klen=t, tq=min(256, t), tk=min(512, t))
        pc3 = pc.reshape(nb, t, 1024)
        pab3 = pab.reshape(nb, t, LANES)
        if tp != t:
            pc3p = jnp.pad(pc3, ((0, 0), (0, tp - t), (0, 0)))
            pab3p = jnp.pad(pab3, ((0, 0), (0, tp - t), (0, 0)))
        else:
            pc3p, pab3p = pc3, pab3
        cst = conv_state[l] if cached else jnp.zeros((nb, CONV_W - 1, C_QKV), F32)
        s0 = gdn_state[l] if cached else jnp.zeros((nb, C_HEADS, C_DK, C_DV), F32)
        yc, s_fin = gdn(pc3p, pab3p, cst, s0, wts["conv_w"][l], wts["gdn_a_log"][l], wts["gdn_dt_bias"][l],
                        wts["gdn_norm_g"][l], t)
        yc = yc[:, :t].reshape(n, C_WIDTH)
        x = out_proj(x, ya, yb.reshape(n, B_WIDTH), yc, mod, wts["w_out"][l], wts["ln1_g"][l], wts["ln1_b"][l], t)
        h2, ids, gates = peer_retrieve(x, mod, wts["peer_wq"][l], wts["peer_keys"][l], t)
        x = peer_mix(h2, ids.T, gates.T, wts["expert_ut"][l], wts["expert_v"][l], x, mod,
                     wts["ln2_g"][l], wts["ln2_b"][l], t)
        new_k.append(k)
        new_v.append(v)
        new_s.append(s_fin)
        xp_tail = jnp.concatenate([cst, pc3[:, :, :C_QKV]], axis=1)[:, -(CONV_W - 1):] if t < CONV_W - 1 \
            else pc3[:, t - (CONV_W - 1):, :C_QKV]
        new_conv.append(xp_tail)
        new_sgu_v.append(va.reshape(nb, t, A_WIDTH))
    k_all = jnp.stack(new_k).reshape(DEPTH, nb, t, B_HEADS, 2 * B_DK)
    v_all = jnp.stack(new_v).reshape(DEPTH, nb, t, B_HEADS, B_DV)
    return (x.reshape(nb, t, D_MODEL), k_all, v_all, jnp.stack(new_s), jnp.stack(new_conv), jnp.stack(new_sgu_v))


def kernel(x_prompt, x_sample, cache_k, cache_v, state_gdn, state_conv, c_prompt, c_sample,
           w_ada, b_ada, w_in, sgu_ln_g, sgu_ln_b, sgu_w, sgu_b,
           lam_q1, lam_k1, lam_q2, lam_k2, diff_norm_g,
           conv_w, gdn_a_log, gdn_dt_bias, gdn_norm_g,
           w_out, ln1_g, ln1_b, peer_wq, peer_keys, expert_u, expert_v, ln2_g, ln2_b):
    nbp = x_prompt.shape[0]
    nbs = x_sample.shape[0]
    wts = dict(
        w_main=w_in[:, :, :P_MAIN].astype(BF16),
        w_ab=jnp.pad(w_in[:, :, P_MAIN:], ((0, 0), (0, 0), (0, LANES - (P_IN - P_MAIN)))).astype(BF16),
        sgu_ln_g=sgu_ln_g, sgu_ln_b=sgu_ln_b, sgu_w=sgu_w, sgu_b=sgu_b,
        lam_q1=lam_q1, lam_k1=lam_k1, lam_q2=lam_q2, lam_k2=lam_k2, diff_norm_g=diff_norm_g,
        conv_w=conv_w, gdn_a_log=gdn_a_log, gdn_dt_bias=gdn_dt_bias, gdn_norm_g=gdn_norm_g,
        w_out=w_out.astype(BF16), ln1_g=ln1_g, ln1_b=ln1_b,
        peer_wq=peer_wq.astype(BF16),
        peer_keys=peer_keys.reshape(DEPTH, 2 * PEER_HEADS, PEER_NKEYS, PEER_DQ // 2).astype(BF16),
        expert_ut=jnp.swapaxes(expert_u.astype(BF16).reshape(DEPTH, PEER_NKEYS // PEER_CI, PEER_CI * PEER_NKEYS, D_MODEL),
                               2, 3),
        expert_v=expert_v.astype(BF16),
        ln2_g=ln2_g, ln2_b=ln2_b,
    )
    mod_all = ada_mod(jnp.concatenate([c_prompt, c_sample], axis=0), w_ada, b_ada)
    mod_all = mod_all.reshape(DEPTH, nbp + nbs, 6, D_MODEL)
    y_p, k_p, v_p, s_p, conv_p, _ = _run_group(x_prompt, mod_all[:, :nbp], None, None, None, None, wts)
    y_s, k_s, v_s, s_s, conv_s, sgu_s = _run_group(x_sample, mod_all[:, nbp:], cache_k, cache_v, state_gdn, state_conv, wts)
    return (y_p, y_s, k_p, v_p, s_p, conv_p, k_s, v_s, s_s, conv_s, sgu_s)
```
